```python
import math
import jax, jax.numpy as jnp
from jax import lax
import numpy as np

D_MODEL = 1024
BATCH = 32
SEQ = 256
DEPTH = 4
DEC_BATCH = 8
DEC_SEQ = 1024
PAST_LEN = 512

GRID_W = 64
HEAD_DIM = 64
A_HEADS = 8
A_KV_HEADS = 2
A_WIDTH = A_HEADS * HEAD_DIM
A_KV_WIDTH = A_KV_HEADS * HEAD_DIM
B_WIDTH = D_MODEL - A_WIDTH
B_CONV_W = 31
C_HEADS = D_MODEL // (2 * HEAD_DIM)
C_VDIM = 2 * HEAD_DIM
C_WIDTH = C_HEADS * C_VDIM
D_FF = 2816
FFN_CONV_W = 3

ROPE_THETA = 10000.0
Q_BLOCK = 128
EPS = 1e-6
ATTN_SCALE = HEAD_DIM ** -0.5
ALPHA = (2 * DEPTH) ** 0.25
BETA = (8 * DEPTH) ** -0.25
N_EVEN = (DEPTH + 1) // 2
N_ODD = DEPTH // 2
EVEN_IN = A_WIDTH + 2 * A_KV_WIDTH + 2 * B_WIDTH
ODD_IN = 3 * C_WIDTH

kernel_name = 'hybrid_diffusion_gqa_conformer_diffattn_step'


def layer_norm(x, g, b):
    xf = x.astype(jnp.float32)
    mu = jnp.mean(xf, axis=-1, keepdims=True)
    var = jnp.mean(jnp.square(xf - mu), axis=-1, keepdims=True)
    return ((xf - mu) * lax.rsqrt(var + EPS) * g + b).astype(x.dtype)


def rms_norm(x, g):
    xf = x.astype(jnp.float32)
    return (xf * lax.rsqrt(jnp.mean(jnp.square(xf), axis=-1, keepdims=True) + EPS) * g).astype(x.dtype)


def rope_tables(n):
    rows = n // GRID_W
    row = jnp.repeat(jnp.arange(rows), GRID_W).astype(jnp.float32)
    col = jnp.tile(jnp.arange(GRID_W), rows).astype(jnp.float32)
    half = HEAD_DIM // 2
    inv = 1.0 / (ROPE_THETA ** (jnp.arange(0, half, 2, dtype=jnp.float32) / half))
    ang_r = row[:, None] * inv
    ang_c = col[:, None] * inv
    return (jnp.cos(ang_r), jnp.sin(ang_r), jnp.cos(ang_c), jnp.sin(ang_c))


def _rotate(x, cos, sin):
    a, b = jnp.split(x, 2, axis=-1)
    cos = cos[None, :, None, :]
    sin = sin[None, :, None, :]
    return jnp.concatenate([a * cos - b * sin, b * cos + a * sin], axis=-1)


def apply_rope(x, tabs):
    cos_r, sin_r, cos_c, sin_c = tabs
    xf = x.astype(jnp.float32)
    xr, xc = jnp.split(xf, 2, axis=-1)
    return jnp.concatenate([_rotate(xr, cos_r, sin_r), _rotate(xc, cos_c, sin_c)], axis=-1).astype(x.dtype)


def _split_q_blocks(q):
    b, s = q.shape[0], q.shape[1]
    qb = Q_BLOCK if s % Q_BLOCK == 0 else s
    return q.reshape(b, s // qb, qb, *q.shape[2:]).swapaxes(0, 1)


def _merge_q_blocks(o):
    o = o.swapaxes(0, 1)
    return o.reshape(o.shape[0], o.shape[1] * o.shape[2], *o.shape[3:])


def gqa_attention(q, k, v):
    b, _, h, d = q.shape
    hkv = k.shape[2]
    g = h // hkv

    def block(qb):
        nq = qb.shape[1]
        qb = qb.reshape(b, nq, hkv, g, d)
        s = jnp.einsum('bqkgd,bskd->bkgqs', qb, k, preferred_element_type=jnp.float32) * ATTN_SCALE
        p = jax.nn.softmax(s, axis=-1).astype(v.dtype)
        o = jnp.einsum('bkgqs,bskd->bqkgd', p, v)
        return o.reshape(b, nq, h, d)

    return _merge_q_blocks(lax.map(block, _split_q_blocks(q)))


def diff_attention(q, k, v, lam):
    def block(qb):
        s = jnp.einsum('bqhcd,bshcd->bchqs', qb, k, preferred_element_type=jnp.float32) * ATTN_SCALE
        p = jax.nn.softmax(s, axis=-1)
        a = (p[:, 0] - lam * p[:, 1]).astype(v.dtype)
        return jnp.einsum('bhqs,bshe->bqhe', a, v)

    return _merge_q_blocks(lax.map(block, _split_q_blocks(q)))


def depthwise_conv(x, w, b):
    y = lax.conv_general_dilated(x, w[:, None, :].astype(x.dtype), window_strides=(1,), padding='SAME',
                                 dimension_numbers=('NWC', 'WIO', 'NWC'), feature_group_count=x.shape[-1])
    return y + b


def even_mixer(h, e, p, pos, ctx_kv):
    b, s, _ = h.shape
    proj = h @ p['w_in_even'][e]
    q, k, v, u = jnp.split(proj, [A_WIDTH, A_WIDTH + A_KV_WIDTH, A_WIDTH + 2 * A_KV_WIDTH], axis=-1)
    q = rms_norm(q.reshape(b, s, A_HEADS, HEAD_DIM), p['q_norm_g'][e])
    k = rms_norm(k.reshape(b, s, A_KV_HEADS, HEAD_DIM), p['k_norm_g'][e])
    v = v.reshape(b, s, A_KV_HEADS, HEAD_DIM)
    if pos is not None:
        q = apply_rope(q, pos)
        k = apply_rope(k, pos)
    if ctx_kv is None:
        k_all, v_all = k, v
    else:
        k_all = jnp.concatenate([k, ctx_kv[0]], axis=1)
        v_all = jnp.concatenate([v, ctx_kv[1]], axis=1)
    attn = gqa_attention(q, k_all, v_all).reshape(b, s, A_WIDTH)
    ua, ub = jnp.split(u, 2, axis=-1)
    conv = depthwise_conv(ua * jax.nn.sigmoid(ub), p['b_conv_w'][e], p['b_conv_b'][e])
    conv = jax.nn.silu(layer_norm(conv, p['b_norm_g'][e], p['b_norm_b'][e]))
    out = jnp.concatenate([attn, conv], axis=-1) @ p['w_out_even'][e]
    return out, (k, v)


def odd_mixer(h, o, layer, p, pos, ctx_kv):
    b, s, _ = h.shape
    q, k, v = jnp.split(h @ p['w_in_odd'][o], 3, axis=-1)
    q = q.reshape(b, s, C_HEADS * 2, HEAD_DIM)
    k = k.reshape(b, s, C_HEADS * 2, HEAD_DIM)
    if pos is not None:
        q = apply_rope(q, pos)
        k = apply_rope(k, pos)
    q = q.reshape(b, s, C_HEADS, 2, HEAD_DIM)
    k = k.reshape(b, s, C_HEADS, 2, HEAD_DIM)
    v = v.reshape(b, s, C_HEADS, C_VDIM)
    if ctx_kv is None:
        k_all, v_all = k, v
    else:
        ck, cv = ctx_kv
        k_all = jnp.concatenate([k, ck.reshape(ck.shape[0], ck.shape[1], C_HEADS, 2, HEAD_DIM)], axis=1)
        v_all = jnp.concatenate([v, cv], axis=1)
    lam_init = 0.8 - 0.6 * math.exp(-0.3 * layer)
    lq1, lk1, lq2, lk2 = p['diff_lambda'][o]
    lam = (jnp.exp(jnp.sum(lq1 * lk1, dtype=jnp.float32)) - jnp.exp(jnp.sum(lq2 * lk2, dtype=jnp.float32))
           + lam_init)
    att = diff_attention(q, k_all, v_all, lam)
    att = rms_norm(att, p['subln_g'][o]) * (1.0 - lam_init)
    out = att.reshape(b, s, C_WIDTH) @ p['w_out_odd'][o]
    return out, (k.reshape(b, s, C_HEADS, C_VDIM), v)


def conv_ffn(h, layer, p):
    u = depthwise_conv(h @ p['w_up'][layer], p['ffn_conv_w'][layer], p['ffn_conv_b'][layer])
    g, v = jnp.split(u, 2, axis=-1)
    return (jax.nn.silu(g) * v) @ p['w_down'][layer]


def trunk_layer(x, mod, layer, p, pos, ctx_kv):
    sh1, sc1, g1, sh2, sc2, g2 = jnp.split(mod[:, None, :], 6, axis=-1)
    hmix = x * (1 + sc1) + sh1
    if layer % 2 == 0:
        mix, kv = even_mixer(hmix, layer // 2, p, pos, ctx_kv)
    else:
        mix, kv = odd_mixer(hmix, layer // 2, layer, p, pos, ctx_kv)
    x = layer_norm(ALPHA * x + g1 * mix, p['ln_g'][layer, 0], p['ln_b'][layer, 0])
    ff = conv_ffn(x * (1 + sc2) + sh2, layer, p)
    x = layer_norm(ALPHA * x + g2 * ff, p['ln_g'][layer, 1], p['ln_b'][layer, 1])
    return x, kv


def setup_inputs(seed: int = 0) -> dict:
    key = jax.random.key(seed)
    keys = iter(jax.random.split(key, 48))
    f32 = jnp.float32

    def nrm(shape, scale):
        return jax.random.normal(next(keys), shape, f32) * scale

    d = D_MODEL
    even_col_scale = jnp.concatenate([jnp.ones((A_WIDTH + A_KV_WIDTH,), f32),
                                      jnp.full((A_KV_WIDTH,), BETA, f32),
                                      jnp.ones((2 * B_WIDTH,), f32)])
    odd_col_scale = jnp.concatenate([jnp.ones((2 * C_WIDTH,), f32), jnp.full((C_WIDTH,), BETA, f32)])
    return {
        'x_prompt': nrm((BATCH, SEQ, d), 1.0),
        'x_sample': nrm((DEC_BATCH, DEC_SEQ, d), 1.0),
        'cache_a_k_l0': nrm((DEC_BATCH, PAST_LEN, A_KV_HEADS, HEAD_DIM), 1.0),
        'cache_a_v_l0': nrm((DEC_BATCH, PAST_LEN, A_KV_HEADS, HEAD_DIM), 0.5),
        'cache_c_k_l1': nrm((DEC_BATCH, PAST_LEN, C_HEADS, C_VDIM), 1.0),
        'cache_c_v_l1': nrm((DEC_BATCH, PAST_LEN, C_HEADS, C_VDIM), 0.5),
        'cache_a_k_l2': nrm((DEC_BATCH, PAST_LEN, A_KV_HEADS, HEAD_DIM), 1.0),
        'cache_a_v_l2': nrm((DEC_BATCH, PAST_LEN, A_KV_HEADS, HEAD_DIM), 0.5),
        'cache_c_k_l3': nrm((DEC_BATCH, PAST_LEN, C_HEADS, C_VDIM), 1.0),
        'cache_c_v_l3': nrm((DEC_BATCH, PAST_LEN, C_HEADS, C_VDIM), 0.5),
        'c': nrm((DEC_BATCH, d), 1.0),
        'c_ctx': nrm((d,), 1.0),
        'w_ada': nrm((DEPTH, d, 6 * d), 0.5 * d ** -0.5),
        'b_ada': nrm((DEPTH, 6 * d), 0.02),
        'ln_g': 1.0 + nrm((DEPTH, 2, d), 0.02),
        'ln_b': nrm((DEPTH, 2, d), 0.02),
        'w_in_even': nrm((N_EVEN, d, EVEN_IN), d ** -0.5) * even_col_scale,
        'q_norm_g': 1.0 + nrm((N_EVEN, HEAD_DIM), 0.02),
        'k_norm_g': 1.0 + nrm((N_EVEN, HEAD_DIM), 0.02),
        'b_conv_w': nrm((N_EVEN, B_CONV_W, B_WIDTH), B_CONV_W ** -0.5),
        'b_conv_b': nrm((N_EVEN, B_WIDTH), 0.02),
        'b_norm_g': 1.0 + nrm((N_EVEN, B_WIDTH), 0.02),
        'b_norm_b': nrm((N_EVEN, B_WIDTH), 0.02),
        'w_out_even': nrm((N_EVEN, A_WIDTH + B_WIDTH, d), BETA * (A_WIDTH + B_WIDTH) ** -0.5),
        'w_in_odd': nrm((N_ODD, d, ODD_IN), d ** -0.5) * odd_col_scale,
        'diff_lambda': nrm((N_ODD, 4, HEAD_DIM), 0.1),
        'subln_g': 1.0 + nrm((N_ODD, C_VDIM), 0.02),
        'w_out_odd': nrm((N_ODD, C_WIDTH, d), BETA * C_WIDTH ** -0.5),
        'w_up': nrm((DEPTH, d, 2 * D_FF), d ** -0.5),
        'ffn_conv_w': nrm((DEPTH, FFN_CONV_W, 2 * D_FF), FFN_CONV_W ** -0.5),
        'ffn_conv_b': nrm((DEPTH, 2 * D_FF), 0.02),
        'w_down': nrm((DEPTH, D_FF, d), BETA * D_FF ** -0.5),
    }


def reference(x_prompt, x_sample, cache_a_k_l0, cache_a_v_l0, cache_c_k_l1, cache_c_v_l1,
              cache_a_k_l2, cache_a_v_l2, cache_c_k_l3, cache_c_v_l3, c, c_ctx,
              w_ada, b_ada, ln_g, ln_b, w_in_even, q_norm_g, k_norm_g, b_conv_w, b_conv_b,
              b_norm_g, b_norm_b, w_out_even, w_in_odd, diff_lambda, subln_g, w_out_odd,
              w_up, ffn_conv_w, ffn_conv_b, w_down):
    p = dict(ln_g=ln_g, ln_b=ln_b, w_in_even=w_in_even, q_norm_g=q_norm_g, k_norm_g=k_norm_g,
             b_conv_w=b_conv_w, b_conv_b=b_conv_b, b_norm_g=b_norm_g, b_norm_b=b_norm_b,
             w_out_even=w_out_even, w_in_odd=w_in_odd, diff_lambda=diff_lambda, subln_g=subln_g,
             w_out_odd=w_out_odd, w_up=w_up, ffn_conv_w=ffn_conv_w, ffn_conv_b=ffn_conv_b, w_down=w_down)
    caches = [(cache_a_k_l0, cache_a_v_l0), (cache_c_k_l1, cache_c_v_l1),
              (cache_a_k_l2, cache_a_v_l2), (cache_c_k_l3, cache_c_v_l3)]
    pos = rope_tables(x_sample.shape[1])
    cond_ctx = jax.nn.silu(c_ctx)[None]
    cond_lat = jax.nn.silu(c)
    y_prompt, y_sample = x_prompt, x_sample
    ctx_state = []
    for layer in range(DEPTH):
        mod_ctx = cond_ctx @ w_ada[layer] + b_ada[layer]
        mod_lat = cond_lat @ w_ada[layer] + b_ada[layer]
        y_prompt, kv = trunk_layer(y_prompt, mod_ctx, layer, p, None, None)
        ctx_state.append(kv)
        y_sample, _ = trunk_layer(y_sample, mod_lat, layer, p, pos, caches[layer])
    (a_k0, a_v0), (c_k1, c_v1), (a_k2, a_v2), (c_k3, c_v3) = ctx_state
    return (y_prompt, y_sample, a_k0, a_v0, c_k1, c_v1, a_k2, a_v2, c_k3, c_v3)
```

```python
import functools
import math

import jax
import jax.numpy as jnp
from jax import lax
from jax.experimental import pallas as pl
from jax.experimental.pallas import tpu as pltpu

F32 = jnp.float32
BF16 = jnp.bfloat16

D_MODEL = 1024
DEPTH = 4
GRID_W = 64
HEAD_DIM = 64
A_HEADS = 8
A_KV_HEADS = 2
A_WIDTH = A_HEADS * HEAD_DIM
A_KV_WIDTH = A_KV_HEADS * HEAD_DIM
B_WIDTH = D_MODEL - A_WIDTH
B_CONV_W = 31
C_HEADS = D_MODEL // (2 * HEAD_DIM)
C_VDIM = 2 * HEAD_DIM
C_WIDTH = C_HEADS * C_VDIM
D_FF = 2816
ROPE_THETA = 10000.0
EPS = 1e-6
ATTN_SCALE = HEAD_DIM ** -0.5
ALPHA = (2 * DEPTH) ** 0.25
EVEN_IN = A_WIDTH + 2 * A_KV_WIDTH + 2 * B_WIDTH
ODD_IN = 3 * C_WIDTH

LANES = 128
MOD_ROWS = 16
VMEM_LIMIT = 48 * 1024 * 1024


def _params(*sem):
    return pltpu.CompilerParams(dimension_semantics=sem, vmem_limit_bytes=VMEM_LIMIT)


def _dot(a, b):
    return jnp.dot(a, b, preferred_element_type=F32)


def _dot_nt(a, b):
    return lax.dot_general(a, b, (((1,), (1,)), ((), ())), preferred_element_type=F32)


def _sigmoid(x):
    return 1.0 / (1.0 + jnp.exp(-x))


def _layer_norm(y, g, b):
    mu = jnp.mean(y, axis=-1, keepdims=True)
    d = y - mu
    var = jnp.mean(d * d, axis=-1, keepdims=True)
    return d * lax.rsqrt(var + EPS) * g + b


def _group_mean_sq(x, group):
    r = lax.broadcasted_iota(jnp.int32, (LANES, LANES), 0) // group
    c = lax.broadcasted_iota(jnp.int32, (LANES, LANES), 1) // group
    ones = jnp.where(r == c, 1.0, 0.0).astype(BF16)
    sq = x * x
    hi = sq.astype(BF16)
    lo = (sq - hi.astype(F32)).astype(BF16)
    return (_dot(hi, ones) + _dot(lo, ones)) * (1.0 / group)


def _rope(x, cos, sin):
    lane = lax.broadcasted_iota(jnp.int32, x.shape, 1)
    partner = jnp.where((lane & 16) == 0, pltpu.roll(x, LANES - 16, 1), pltpu.roll(x, 16, 1))
    return x * cos + partner * sin


def _mod_kernel(c_ref, w_ref, b_ref, o_ref):
    c = c_ref[...]
    s = (c * _sigmoid(c)).astype(BF16)
    o_ref[...] = _dot(s, w_ref[...].astype(BF16)) + b_ref[...]


def _modulation(cond, w_ada, b_ada):
    tn = 1536
    n = 6 * D_MODEL
    return pl.pallas_call(
        _mod_kernel,
        grid=(DEPTH, n // tn),
        in_specs=[
            pl.BlockSpec((MOD_ROWS, D_MODEL), lambda l, j: (0, 0)),
            pl.BlockSpec((None, D_MODEL, tn), lambda l, j: (l, 0, j)),
            pl.BlockSpec((None, 1, tn), lambda l, j: (l, 0, j)),
        ],
        out_specs=pl.BlockSpec((None, MOD_ROWS, tn), lambda l, j: (l, 0, j)),
        out_shape=jax.ShapeDtypeStruct((DEPTH, MOD_ROWS, n), F32),
        compiler_params=_params("arbitrary", "arbitrary"), name="adaln_mod",
    )(cond, w_ada, b_ada.reshape(DEPTH, 1, n))


def _mod_spec(layer, which, tiles_per_seq, latent):
    def index(i, *_):
        row = 1 + i // tiles_per_seq if latent else 0
        return ((layer * MOD_ROWS + row) * 6 + which, 0, 0)
    return pl.BlockSpec((None, 1, D_MODEL), index)


def _in_even_kernel(*refs, latent):
    if latent:
        x_ref, sh_ref, sc_ref, w_ref, qg_ref, kg_ref, cos_ref, sin_ref, q_ref, k_ref, v_ref, u_ref = refs
    else:
        x_ref, sh_ref, sc_ref, w_ref, qg_ref, kg_ref, q_ref, k_ref, v_ref, u_ref, ko_ref, vo_ref = refs
    h = (x_ref[...] * (1.0 + sc_ref[...]) + sh_ref[...]).astype(BF16)
    proj = _dot(h, w_ref[...])
    for c in range((A_WIDTH + A_KV_WIDTH) // LANES):
        lo = c * LANES
        x = proj[:, lo:lo + LANES]
        is_q = lo < A_WIDTH
        gain = qg_ref[...] if is_q else kg_ref[...]
        x = x * lax.rsqrt(_group_mean_sq(x, HEAD_DIM) + EPS) * gain
        if not latent and not is_q:
            ko_ref[...] = x
        if latent:
            x = _rope(x, cos_ref[...], sin_ref[...])
        if is_q:
            q_ref[:, lo:lo + LANES] = (x * ATTN_SCALE).astype(BF16)
        else:
            k_ref[...] = x.astype(BF16)
    v = proj[:, A_WIDTH + A_KV_WIDTH:A_WIDTH + 2 * A_KV_WIDTH]
    v_ref[...] = v.astype(BF16)
    if not latent:
        vo_ref[...] = v
    u0 = A_WIDTH + 2 * A_KV_WIDTH
    ua = proj[:, u0:u0 + B_WIDTH]
    ub = proj[:, u0 + B_WIDTH:u0 + 2 * B_WIDTH]
    u_ref[...] = ua * _sigmoid(ub)


def _in_even(x, mod, layer, w_in, qg, kg, rope, *, seq_len, latent):
    rows = x.shape[0]
    tm = 256
    tps = seq_len // tm
    row_spec = lambda w: pl.BlockSpec((tm, w), lambda i: (i, 0))
    full = lambda a: pl.BlockSpec(a.shape, lambda i: (0,) * a.ndim)
    in_specs = [row_spec(D_MODEL), _mod_spec(layer, 0, tps, latent), _mod_spec(layer, 1, tps, latent),
                full(w_in), full(qg), full(kg)]
    args = [x, mod, mod, w_in, qg, kg]
    out_specs = [row_spec(A_WIDTH), row_spec(A_KV_WIDTH), row_spec(A_KV_WIDTH), row_spec(B_WIDTH)]
    out_shape = [jax.ShapeDtypeStruct((rows, A_WIDTH), BF16), jax.ShapeDtypeStruct((rows, A_KV_WIDTH), BF16),
                 jax.ShapeDtypeStruct((rows, A_KV_WIDTH), BF16), jax.ShapeDtypeStruct((rows, B_WIDTH), F32)]
    if latent:
        pos_spec = pl.BlockSpec((tm, LANES), lambda i: (i % tps, 0))
        in_specs += [pos_spec, pos_spec]
        args += list(rope)
    else:
        out_specs += [row_spec(A_KV_WIDTH), row_spec(A_KV_WIDTH)]
        out_shape += [jax.ShapeDtypeStruct((rows, A_KV_WIDTH), F32)] * 2
    return pl.pallas_call(
        functools.partial(_in_even_kernel, latent=latent),
        grid=(rows // tm,), in_specs=in_specs, out_specs=out_specs, out_shape=out_shape,
        compiler_params=_params("arbitrary"), name="in_even",
    )(*args)


def _in_odd_kernel(*refs, latent):
    if latent:
        x_ref, sh_ref, sc_ref, w_ref, cos_ref, sin_ref, q_ref, k_ref, v_ref = refs
    else:
        x_ref, sh_ref, sc_ref, w_ref, q_ref, k_ref, v_ref, ko_ref, vo_ref = refs
    h = (x_ref[...] * (1.0 + sc_ref[...]) + sh_ref[...]).astype(BF16)
    proj = _dot(h, w_ref[...])
    for c in range(2 * C_WIDTH // LANES):
        lo = c * LANES
        x = proj[:, lo:lo + LANES]
        if latent:
            x = _rope(x, cos_ref[...], sin_ref[...])
        if lo < C_WIDTH:
            q_ref[:, lo:lo + LANES] = (x * ATTN_SCALE).astype(BF16)
        else:
            k_ref[:, lo - C_WIDTH:lo - C_WIDTH + LANES] = x.astype(BF16)
    v = proj[:, 2 * C_WIDTH:]
    v_ref[...] = v.astype(BF16)
    if not latent:
        ko_ref[...] = proj[:, C_WIDTH:2 * C_WIDTH]
        vo_ref[...] = v


def _in_odd(x, mod, layer, w_in, rope, *, seq_len, latent):
    rows = x.shape[0]
    tm = 256
    tps = seq_len // tm
    row_spec = lambda w: pl.BlockSpec((tm, w), lambda i: (i, 0))
    in_specs = [row_spec(D_MODEL), _mod_spec(layer, 0, tps, latent), _mod_spec(layer, 1, tps, latent),
                pl.BlockSpec(w_in.shape, lambda i: (0, 0))]
    args = [x, mod, mod, w_in]
    out_specs = [row_spec(C_WIDTH)] * 3
    out_shape = [jax.ShapeDtypeStruct((rows, C_WIDTH), BF16)] * 3
    if latent:
        pos_spec = pl.BlockSpec((tm, LANES), lambda i: (i % tps, 0))
        in_specs += [pos_spec, pos_spec]
        args += list(rope)
    else:
        out_specs += [row_spec(C_WIDTH)] * 2
        out_shape += [jax.ShapeDtypeStruct((rows, C_WIDTH), F32)] * 2
    return pl.pallas_call(
        functools.partial(_in_odd_kernel, latent=latent),
        grid=(rows // tm,), in_specs=in_specs, out_specs=out_specs, out_shape=out_shape,
        compiler_params=_params("arbitrary"), name="in_odd",
    )(*args)


def _conv_b_kernel(u_ref, w_ref, b_ref, g_ref, beta_ref, o_ref, pad_ref, *, seq_len):
    halo = 16
    chunk = 128
    zeros = jnp.zeros((halo, B_WIDTH), F32)
    pad_ref[0:halo, :] = zeros
    pad_ref[halo + seq_len:2 * halo + seq_len, :] = zeros
    pad_ref[halo:halo + seq_len, :] = u_ref[...]
    off = halo - B_CONV_W // 2
    for r in range(seq_len // chunk):
        cols = []
        for c in range(B_WIDTH // LANES):
            lanes = slice(c * LANES, (c + 1) * LANES)
            acc = jnp.zeros((chunk, LANES), F32) + b_ref[:, lanes]
            for k in range(B_CONV_W):
                start = r * chunk + off + k
                acc = acc + pad_ref[start:start + chunk, lanes] * w_ref[k:k + 1, lanes]
            cols.append(acc)
        y = _layer_norm(jnp.concatenate(cols, axis=-1), g_ref[...], beta_ref[...])
        o_ref[r * chunk:(r + 1) * chunk, :] = (y * _sigmoid(y)).astype(BF16)


def _conv_b(u, w, b, g, beta, *, seq_len):
    rows = u.shape[0]
    full = lambda a: pl.BlockSpec(a.shape, lambda i: (0,) * a.ndim)
    return pl.pallas_call(
        functools.partial(_conv_b_kernel, seq_len=seq_len),
        grid=(rows // seq_len,),
        in_specs=[pl.BlockSpec((seq_len, B_WIDTH), lambda i: (i, 0)), full(w), full(b), full(g), full(beta)],
        out_specs=pl.BlockSpec((seq_len, B_WIDTH), lambda i: (i, 0)),
        out_shape=jax.ShapeDtypeStruct((rows, B_WIDTH), BF16),
        scratch_shapes=[pltpu.VMEM((seq_len + 32, B_WIDTH), F32)],
        compiler_params=_params("arbitrary"), name="conv_b",
    )(u, w, b, g, beta)


def _softmax_parts(q, keys):
    s = [_dot_nt(q, k) for k in keys]
    m = s[0].max(axis=-1, keepdims=True)
    for x in s[1:]:
        m = jnp.maximum(m, x.max(axis=-1, keepdims=True))
    p = [jnp.exp(x - m) for x in s]
    l = p[0].sum(axis=-1, keepdims=True)
    for x in p[1:]:
        l = l + x.sum(axis=-1, keepdims=True)
    return p, 1.0 / l


def _attn_even_kernel(*refs, latent):
    if latent:
        q_ref, k_ref, v_ref, ck_ref, cv_ref, o_ref = refs
    else:
        q_ref, k_ref, v_ref, o_ref = refs
    outs = []
    for j in range(A_KV_HEADS):
        lanes = slice(j * HEAD_DIM, (j + 1) * HEAD_DIM)
        keys = [k_ref[:, lanes]]
        vals = [v_ref[:, lanes]]
        if latent:
            keys.append(ck_ref[:, lanes].astype(BF16))
            vals.append(cv_ref[:, lanes].astype(BF16))
        for g in range(A_HEADS // A_KV_HEADS):
            h = j * (A_HEADS // A_KV_HEADS) + g
            p, inv = _softmax_parts(q_ref[:, h * HEAD_DIM:(h + 1) * HEAD_DIM], keys)
            o = _dot(p[0].astype(BF16), vals[0])
            for pp, vv in zip(p[1:], vals[1:]):
                o = o + _dot(pp.astype(BF16), vv)
            outs.append(o * inv)
    o_ref[...] = jnp.concatenate(outs, axis=-1).astype(BF16)


def _attn_even(q, k, v, cache, *, seq_len, latent):
    rows = q.shape[0]
    tq = 256
    tps = seq_len // tq
    q_spec = pl.BlockSpec((tq, A_WIDTH), lambda b, i: (b * tps + i, 0))
    kv_spec = pl.BlockSpec((seq_len, A_KV_WIDTH), lambda b, i: (b, 0))
    in_specs = [q_spec, kv_spec, kv_spec]
    args = [q, k, v]
    if latent:
        past = cache[0].shape[1]
        c_spec = pl.BlockSpec((None, past, A_KV_WIDTH), lambda b, i: (b, 0, 0))
        in_specs += [c_spec, c_spec]
        args += [cache[0].reshape(-1, past, A_KV_WIDTH), cache[1].reshape(-1, past, A_KV_WIDTH)]
    return pl.pallas_call(
        functools.partial(_attn_even_kernel, latent=latent),
        grid=(rows // seq_len, tps), in_specs=in_specs, out_specs=q_spec,
        out_shape=jax.ShapeDtypeStruct((rows, A_WIDTH), BF16),
        compiler_params=_params("arbitrary", "arbitrary"), name="attn_even",
    )(*args)


def _attn_odd_kernel(*refs, latent, lam_init):
    if latent:
        q_ref, k_ref, v_ref, ck_ref, cv_ref, dl_ref, sg_ref, o_ref = refs
    else:
        q_ref, k_ref, v_ref, dl_ref, sg_ref, o_ref = refs
    dl = dl_ref[...]
    lam = (jnp.exp(jnp.sum(dl[0:1] * dl[1:2], axis=-1, keepdims=True))
           - jnp.exp(jnp.sum(dl[2:3] * dl[3:4], axis=-1, keepdims=True)) + lam_init)
    vals = [v_ref[...]]
    if latent:
        vals.append(cv_ref[...].astype(BF16))
    o = None
    for c in range(2):
        lanes = slice(c * HEAD_DIM, (c + 1) * HEAD_DIM)
        keys = [k_ref[:, lanes]]
        if latent:
            keys.append(ck_ref[:, lanes].astype(BF16))
        p, inv = _softmax_parts(q_ref[:, lanes], keys)
        oc = _dot(p[0].astype(BF16), vals[0])
        for pp, vv in zip(p[1:], vals[1:]):
            oc = oc + _dot(pp.astype(BF16), vv)
        oc = oc * inv
        o = oc if c == 0 else o - lam * oc
    ms = jnp.mean(o * o, axis=-1, keepdims=True)
    o_ref[...] = (o * lax.rsqrt(ms + EPS) * sg_ref[...] * (1.0 - lam_init)).astype(BF16)


def _attn_odd(q, k, v, cache, dl, sg, *, seq_len, latent, lam_init):
    rows = q.shape[0]
    tq = 256
    tps = seq_len // tq
    q_spec = pl.BlockSpec((tq, C_VDIM), lambda b, h, i: (b * tps + i, h))
    kv_spec = pl.BlockSpec((seq_len, C_VDIM), lambda b, h, i: (b, h))
    full = lambda a: pl.BlockSpec(a.shape, lambda b, h, i: (0,) * a.ndim)
    in_specs = [q_spec, kv_spec, kv_spec]
    args = [q, k, v]
    if latent:
        past = cache[0].shape[1]
        c_spec = pl.BlockSpec((None, past, C_VDIM), lambda b, h, i: (b, 0, h))
        in_specs += [c_spec, c_spec]
        args += [cache[0].reshape(-1, past, C_WIDTH), cache[1].reshape(-1, past, C_WIDTH)]
    in_specs += [full(dl), full(sg)]
    args += [dl, sg]
    return pl.pallas_call(
        functools.partial(_attn_odd_kernel, latent=latent, lam_init=lam_init),
        grid=(rows // seq_len, C_HEADS, tps), in_specs=in_specs, out_specs=q_spec,
        out_shape=jax.ShapeDtypeStruct((rows, C_WIDTH), BF16),
        compiler_params=_params("arbitrary", "arbitrary", "arbitrary"), name="attn_odd",
    )(*args)


def _out_kernel(*refs, n_parts):
    parts = refs[:n_parts]
    w_ref, x_ref, g_ref, lng_ref, lnb_ref, o_ref = refs[n_parts:]
    width = w_ref.shape[0] // n_parts
    mix = _dot(parts[0][...], w_ref[0:width, :])
    for j in range(1, n_parts):
        mix = mix + _dot(parts[j][...], w_ref[j * width:(j + 1) * width, :])
    y = ALPHA * x_ref[...] + g_ref[...] * mix
    o_ref[...] = _layer_norm(y, lng_ref[...], lnb_ref[...])


def _out_proj(parts, w_out, x, mod, layer, ln_g, ln_b, *, seq_len, latent):
    rows = x.shape[0]
    tm = 256
    tps = seq_len // tm
    row_spec = lambda w: pl.BlockSpec((tm, w), lambda i: (i, 0))
    full = lambda a: pl.BlockSpec(a.shape, lambda i: (0,) * a.ndim)
    in_specs = [row_spec(p.shape[1]) for p in parts]
    in_specs += [full(w_out), row_spec(D_MODEL), _mod_spec(layer, 2, tps, latent), full(ln_g), full(ln_b)]
    return pl.pallas_call(
        functools.partial(_out_kernel, n_parts=len(parts)),
        grid=(rows // tm,), in_specs=in_specs, out_specs=row_spec(D_MODEL),
        out_shape=jax.ShapeDtypeStruct((rows, D_MODEL), F32),
        compiler_params=_params("arbitrary"), name="out_proj_ln",
    )(*parts, w_out, x, mod, ln_g, ln_b)


def _ffn_kernel(x_ref, sh_ref, sc_ref, g_ref, lng_ref, lnb_ref, wg_ref, wv_ref, cwg_ref, cwv_ref,
                cbg_ref, cbv_ref, wd_ref, o_ref, h_ref, acc_ref, *, seq_len):
    f = pl.program_id(1)

    @pl.when(f == 0)
    def _():
        h_ref[...] = (x_ref[...] * (1.0 + sc_ref[...]) + sh_ref[...]).astype(BF16)
        acc_ref[...] = jnp.zeros_like(acc_ref)

    h = h_ref[...]
    tm = h.shape[0]
    tf = wg_ref.shape[1]
    pos = lax.broadcasted_iota(jnp.int32, (tm, tf), 0) % seq_len
    first = pos == 0
    last = pos == seq_len - 1

    def conv(u, cw_ref, cb_ref):
        prev = jnp.where(first, 0.0, pltpu.roll(u, 1, 0))
        nxt = jnp.where(last, 0.0, pltpu.roll(u, tm - 1, 0))
        return prev * cw_ref[0:1, :] + u * cw_ref[1:2, :] + nxt * cw_ref[2:3, :] + cb_ref[...]

    gate = conv(_dot(h, wg_ref[...]), cwg_ref, cbg_ref)
    val = conv(_dot(h, wv_ref[...]), cwv_ref, cbv_ref)
    a = (gate * _sigmoid(gate) * val).astype(BF16)
    acc_ref[...] += _dot(a, wd_ref[...])

    @pl.when(f == pl.num_programs(1) - 1)
    def _():
        y = ALPHA * x_ref[...] + g_ref[...] * acc_ref[...]
        o_ref[...] = _layer_norm(y, lng_ref[...], lnb_ref[...])


def _ffn(x, mod, layer, ln_g, ln_b, w_up, cw, cb, w_down, *, seq_len, latent):
    rows = x.shape[0]
    tm = 1024
    tf = 256
    nf = D_FF // tf
    tps = max(seq_len // tm, 1)
    row_spec = pl.BlockSpec((tm, D_MODEL), lambda i, f: (i, 0))
    full = lambda a: pl.BlockSpec(a.shape, lambda i, f: (0,) * a.ndim)
    lat_tiles = tps if latent else 1
    in_specs = [
        row_spec, _mod_spec(layer, 3, lat_tiles, latent), _mod_spec(layer, 4, lat_tiles, latent),
        _mod_spec(layer, 5, lat_tiles, latent), full(ln_g), full(ln_b),
        pl.BlockSpec((None, D_MODEL, tf), lambda i, f: (layer, 0, f)),
        pl.BlockSpec((None, D_MODEL, tf), lambda i, f: (layer, 0, nf + f)),
        pl.BlockSpec((None, 3, tf), lambda i, f: (layer, 0, f)),
        pl.BlockSpec((None, 3, tf), lambda i, f: (layer, 0, nf + f)),
        pl.BlockSpec((None, 1, tf), lambda i, f: (layer, 0, f)),
        pl.BlockSpec((None, 1, tf), lambda i, f: (layer, 0, nf + f)),
        pl.BlockSpec((None, tf, D_MODEL), lambda i, f: (layer, f, 0)),
    ]
    return pl.pallas_call(
        functools.partial(_ffn_kernel, seq_len=seq_len),
        grid=(rows // tm, nf), in_specs=in_specs, out_specs=row_spec,
        out_shape=jax.ShapeDtypeStruct((rows, D_MODEL), F32),
        scratch_shapes=[pltpu.VMEM((tm, D_MODEL), BF16), pltpu.VMEM((tm, D_MODEL), F32)],
        compiler_params=_params("arbitrary", "arbitrary"), name="conv_ffn",
    )(x, mod, mod, mod, ln_g, ln_b, w_up, w_up, cw, cw, cb, cb, w_down)


def _rope_lane_tables(n):
    t = jnp.arange(n)
    half = HEAD_DIM // 2
    inv = 1.0 / (ROPE_THETA ** (jnp.arange(0, half, 2, dtype=F32) / half))
    ang_r = (t // GRID_W).astype(F32)[:, None] * inv
    ang_c = (t % GRID_W).astype(F32)[:, None] * inv
    cos = jnp.concatenate([jnp.cos(ang_r)] * 2 + [jnp.cos(ang_c)] * 2, axis=-1)
    sin = jnp.concatenate([-jnp.sin(ang_r), jnp.sin(ang_r), -jnp.sin(ang_c), jnp.sin(ang_c)], axis=-1)
    reps = LANES // HEAD_DIM
    return jnp.tile(cos, (1, reps)), jnp.tile(sin, (1, reps))


def kernel(x_prompt, x_sample, cache_a_k_l0, cache_a_v_l0, cache_c_k_l1, cache_c_v_l1, cache_a_k_l2,
           cache_a_v_l2, cache_c_k_l3, cache_c_v_l3, c, c_ctx, w_ada, b_ada, ln_g, ln_b, w_in_even,
           q_norm_g, k_norm_g, b_conv_w, b_conv_b, b_norm_g, b_norm_b, w_out_even, w_in_odd,
           diff_lambda, subln_g, w_out_odd, w_up, ffn_conv_w, ffn_conv_b, w_down):
    batch, seq, d = x_prompt.shape
    dec_batch, dec_seq, _ = x_sample.shape
    assert d == D_MODEL and 1 + dec_batch <= MOD_ROWS
    caches = [(cache_a_k_l0, cache_a_v_l0), (cache_c_k_l1, cache_c_v_l1),
              (cache_a_k_l2, cache_a_v_l2), (cache_c_k_l3, cache_c_v_l3)]

    cond = jnp.zeros((MOD_ROWS, d), F32).at[0].set(c_ctx).at[1:1 + dec_batch].set(c)
    mod = _modulation(cond, w_ada, b_ada).reshape(DEPTH * MOD_ROWS * 6, 1, d)
    rope = _rope_lane_tables(dec_seq)

    w_in_even_b = w_in_even.astype(BF16)
    w_out_even_b = w_out_even.astype(BF16)
    w_in_odd_b = w_in_odd.astype(BF16)
    w_out_odd_b = w_out_odd.astype(BF16)
    w_up_b = w_up.astype(BF16)
    w_down_b = w_down.astype(BF16)
    cb = ffn_conv_b.reshape(DEPTH, 1, 2 * D_FF)

    streams = [
        [x_prompt.reshape(batch * seq, d), seq, False],
        [x_sample.reshape(dec_batch * dec_seq, d), dec_seq, True],
    ]
    ctx_state = []
    for layer in range(DEPTH):
        lng = ln_g[layer].reshape(2, 1, d)
        lnb = ln_b[layer].reshape(2, 1, d)
        for stream in streams:
            x, s, latent = stream
            kw = dict(seq_len=s, latent=latent)
            if layer % 2 == 0:
                e = layer // 2
                qg = jnp.tile(q_norm_g[e], LANES // HEAD_DIM)[None]
                kg = jnp.tile(k_norm_g[e], LANES // HEAD_DIM)[None]
                res = _in_even(x, mod, layer, w_in_even_b[e], qg, kg, rope, **kw)
                q, k, v, u = res[:4]
                if not latent:
                    ctx_state.append((res[4].reshape(batch, seq, A_KV_HEADS, HEAD_DIM),
                                      res[5].reshape(batch, seq, A_KV_HEADS, HEAD_DIM)))
                attn = _attn_even(q, k, v, caches[layer], **kw)
                conv = _conv_b(u, b_conv_w[e], b_conv_b[e][None], b_norm_g[e][None], b_norm_b[e][None],
                               seq_len=s)
                parts, w_out = [attn, conv], w_out_even_b[e]
            else:
                o = layer // 2
                res = _in_odd(x, mod, layer, w_in_odd_b[o], rope, **kw)
                q, k, v = res[:3]
                if not latent:
                    ctx_state.append((res[3].reshape(batch, seq, C_HEADS, C_VDIM),
                                      res[4].reshape(batch, seq, C_HEADS, C_VDIM)))
                lam_init = 0.8 - 0.6 * math.exp(-0.3 * layer)
                attn = _attn_odd(q, k, v, caches[layer], diff_lambda[o], subln_g[o][None],
                                 lam_init=lam_init, **kw)
                parts, w_out = [attn], w_out_odd_b[o]
            x = _out_proj(parts, w_out, x, mod, layer, lng[0], lnb[0], **kw)
            x = _ffn(x, mod, layer, lng[1], lnb[1], w_up_b, ffn_conv_w, cb, w_down_b, **kw)
            stream[0] = x
    y_prompt = streams[0][0].reshape(batch, seq, d)
    y_sample = streams[1][0].reshape(dec_batch, dec_seq, d)
    (a_k0, a_v0), (c_k1, c_v1), (a_k2, a_v2), (c_k3, c_v3) = ctx_state
    return (y_prompt, y_sample, a_k0, a_v0, c_k1, c_v1, a_k2, a_v2, c_k3, c_v3)
```

```python
import functools
import math

import jax
import jax.numpy as jnp
from jax import lax
from jax.experimental import pallas as pl
from jax.experimental.pallas import tpu as pltpu

F32 = jnp.float32
BF16 = jnp.bfloat16

D_MODEL = 1024
DEPTH = 4
GRID_W = 64
HEAD_DIM = 64
A_HEADS = 8
A_KV_HEADS = 2
A_WIDTH = A_HEADS * HEAD_DIM
A_KV_WIDTH = A_KV_HEADS * HEAD_DIM
B_WIDTH = D_MODEL - A_WIDTH
B_CONV_W = 31
C_HEADS = D_MODEL // (2 * HEAD_DIM)
C_VDIM = 2 * HEAD_DIM
C_WIDTH = C_HEADS * C_VDIM
D_FF = 2816
ROPE_THETA = 10000.0
EPS = 1e-6
ATTN_SCALE = HEAD_DIM ** -0.5
ALPHA = (2 * DEPTH) ** 0.25
EVEN_IN = A_WIDTH + 2 * A_KV_WIDTH + 2 * B_WIDTH
ODD_IN = 3 * C_WIDTH

LANES = 128
MXU_DIM = 256
FF_CHUNK = MXU_DIM
FF_CHUNKS = D_FF // FF_CHUNK
MOD_ROWS = 16
VMEM_LIMIT = 56 * 1024 * 1024


def _params(*sem):
    return pltpu.CompilerParams(dimension_semantics=sem, vmem_limit_bytes=VMEM_LIMIT)


def _dot(a, b):
    return jnp.dot(a, b, preferred_element_type=F32)


def _dot_nt(a, b):
    return lax.dot_general(a, b, (((1,), (1,)), ((), ())), preferred_element_type=F32)


def _sigmoid(x):
    return 1.0 / (1.0 + jnp.exp(-x))


def _layer_norm(y, g, b):
    mu = jnp.mean(y, axis=-1, keepdims=True)
    d = y - mu
    var = jnp.mean(d * d, axis=-1, keepdims=True)
    return d * lax.rsqrt(var + EPS) * g + b


def _group_mean_sq(x, group):
    r = lax.broadcasted_iota(jnp.int32, (LANES, LANES), 0) // group
    c = lax.broadcasted_iota(jnp.int32, (LANES, LANES), 1) // group
    ones = jnp.where(r == c, 1.0, 0.0).astype(BF16)
    sq = x * x
    hi = sq.astype(BF16)
    lo = (sq - hi.astype(F32)).astype(BF16)
    return (_dot(hi, ones) + _dot(lo, ones)) * (1.0 / group)


def _rope(x, cos, sin):
    lane = lax.broadcasted_iota(jnp.int32, x.shape, 1)
    partner = jnp.where((lane & 16) == 0, pltpu.roll(x, LANES - 16, 1), pltpu.roll(x, 16, 1))
    return x * cos + partner * sin


def _mod_kernel(c_ref, w_ref, b_ref, o_ref):
    c = c_ref[...]
    s = (c * _sigmoid(c)).astype(BF16)
    o_ref[...] = _dot(s, w_ref[...].astype(BF16)) + b_ref[...]


def _modulation(cond, w_ada, b_ada):
    tn = 1536
    n = 6 * D_MODEL
    return pl.pallas_call(
        _mod_kernel,
        grid=(DEPTH, n // tn),
        in_specs=[
            pl.BlockSpec((MOD_ROWS, D_MODEL), lambda l, j: (0, 0)),
            pl.BlockSpec((None, D_MODEL, tn), lambda l, j: (l, 0, j)),
            pl.BlockSpec((None, 1, tn), lambda l, j: (l, 0, j)),
        ],
        out_specs=pl.BlockSpec((None, MOD_ROWS, tn), lambda l, j: (l, 0, j)),
        out_shape=jax.ShapeDtypeStruct((DEPTH, MOD_ROWS, n), F32),
        compiler_params=_params("arbitrary", "arbitrary"), name="adaln_mod",
    )(cond, w_ada, b_ada.reshape(DEPTH, 1, n))


def _mod_spec(layer, which, tiles_per_seq, latent):
    def index(i, *_):
        row = 1 + i // tiles_per_seq if latent else 0
        return ((layer * MOD_ROWS + row) * 6 + which, 0, 0)
    return pl.BlockSpec((None, 1, D_MODEL), index)


def _in_even_kernel(*refs, latent):
    if latent:
        x_ref, sh_ref, sc_ref, w_ref, qg_ref, kg_ref, cos_ref, sin_ref, q_ref, k_ref, v_ref, u_ref = refs
    else:
        x_ref, sh_ref, sc_ref, w_ref, qg_ref, kg_ref, q_ref, k_ref, v_ref, u_ref, ko_ref, vo_ref = refs
    h = (x_ref[...] * (1.0 + sc_ref[...]) + sh_ref[...]).astype(BF16)
    proj = _dot(h, w_ref[...])
    for c in range((A_WIDTH + A_KV_WIDTH) // LANES):
        lo = c * LANES
        x = proj[:, lo:lo + LANES]
        is_q = lo < A_WIDTH
        gain = qg_ref[...] if is_q else kg_ref[...]
        x = x * lax.rsqrt(_group_mean_sq(x, HEAD_DIM) + EPS) * gain
        if not latent and not is_q:
            ko_ref[...] = x
        if latent:
            x = _rope(x, cos_ref[...], sin_ref[...])
        if is_q:
            q_ref[:, lo:lo + LANES] = (x * ATTN_SCALE).astype(BF16)
        else:
            k_ref[:, 0:LANES] = x.astype(BF16)
            k_ref[:, LANES:2 * LANES] = pltpu.roll(x, HEAD_DIM, 1).astype(BF16)
    v = proj[:, A_WIDTH + A_KV_WIDTH:A_WIDTH + 2 * A_KV_WIDTH]
    v_ref[...] = v.T.astype(BF16)
    if not latent:
        vo_ref[...] = v
    u0 = A_WIDTH + 2 * A_KV_WIDTH
    ua = proj[:, u0:u0 + B_WIDTH]
    ub = proj[:, u0 + B_WIDTH:u0 + 2 * B_WIDTH]
    u_ref[...] = ua * _sigmoid(ub)


def _in_even(x, mod, layer, w_in, qg, kg, rope, *, seq_len, latent):
    rows = x.shape[0]
    tm = 256
    tps = seq_len // tm
    row_spec = lambda w: pl.BlockSpec((tm, w), lambda i: (i, 0))
    full = lambda a: pl.BlockSpec(a.shape, lambda i: (0,) * a.ndim)
    in_specs = [row_spec(D_MODEL), _mod_spec(layer, 0, tps, latent), _mod_spec(layer, 1, tps, latent),
                full(w_in), full(qg), full(kg)]
    args = [x, mod, mod, w_in, qg, kg]
    out_specs = [row_spec(A_WIDTH), row_spec(2 * A_KV_WIDTH),
                 pl.BlockSpec((A_KV_WIDTH, tm), lambda i: (0, i)), row_spec(B_WIDTH)]
    out_shape = [jax.ShapeDtypeStruct((rows, A_WIDTH), BF16), jax.ShapeDtypeStruct((rows, 2 * A_KV_WIDTH), BF16),
                 jax.ShapeDtypeStruct((A_KV_WIDTH, rows), BF16), jax.ShapeDtypeStruct((rows, B_WIDTH), F32)]
    if latent:
        pos_spec = pl.BlockSpec((tm, LANES), lambda i: (i % tps, 0))
        in_specs += [pos_spec, pos_spec]
        args += list(rope)
    else:
        out_specs += [row_spec(A_KV_WIDTH), row_spec(A_KV_WIDTH)]
        out_shape += [jax.ShapeDtypeStruct((rows, A_KV_WIDTH), F32)] * 2
    return pl.pallas_call(
        functools.partial(_in_even_kernel, latent=latent),
        grid=(rows // tm,), in_specs=in_specs, out_specs=out_specs, out_shape=out_shape,
        compiler_params=_params("arbitrary"), name="in_even",
    )(*args)


def _in_odd_kernel(*refs, latent):
    if latent:
        x_ref, sh_ref, sc_ref, w_ref, cos_ref, sin_ref, q_ref, k_ref, v_ref = refs
    else:
        x_ref, sh_ref, sc_ref, w_ref, q_ref, k_ref, v_ref, ko_ref, vo_ref = refs
    h = (x_ref[...] * (1.0 + sc_ref[...]) + sh_ref[...]).astype(BF16)
    proj = _dot(h, w_ref[...])
    for c in range(2 * C_WIDTH // LANES):
        lo = c * LANES
        x = proj[:, lo:lo + LANES]
        if latent:
            x = _rope(x, cos_ref[...], sin_ref[...])
        if lo < C_WIDTH:
            q_ref[:, lo:lo + LANES] = (x * ATTN_SCALE).astype(BF16)
        else:
            k_ref[:, lo - C_WIDTH:lo - C_WIDTH + LANES] = x.astype(BF16)
    v = proj[:, 2 * C_WIDTH:]
    v_ref[...] = v.T.astype(BF16)
    if not latent:
        ko_ref[...] = proj[:, C_WIDTH:2 * C_WIDTH]
        vo_ref[...] = v


def _in_odd(x, mod, layer, w_in, rope, *, seq_len, latent):
    rows = x.shape[0]
    tm = 256
    tps = seq_len // tm
    row_spec = lambda w: pl.BlockSpec((tm, w), lambda i: (i, 0))
    in_specs = [row_spec(D_MODEL), _mod_spec(layer, 0, tps, latent), _mod_spec(layer, 1, tps, latent),
                pl.BlockSpec(w_in.shape, lambda i: (0, 0))]
    args = [x, mod, mod, w_in]
    out_specs = [row_spec(C_WIDTH)] * 2 + [pl.BlockSpec((C_WIDTH, tm), lambda i: (0, i))]
    out_shape = [jax.ShapeDtypeStruct((rows, C_WIDTH), BF16)] * 2 + [jax.ShapeDtypeStruct((C_WIDTH, rows), BF16)]
    if latent:
        pos_spec = pl.BlockSpec((tm, LANES), lambda i: (i % tps, 0))
        in_specs += [pos_spec, pos_spec]
        args += list(rope)
    else:
        out_specs += [row_spec(C_WIDTH)] * 2
        out_shape += [jax.ShapeDtypeStruct((rows, C_WIDTH), F32)] * 2
    return pl.pallas_call(
        functools.partial(_in_odd_kernel, latent=latent),
        grid=(rows // tm,), in_specs=in_specs, out_specs=out_specs, out_shape=out_shape,
        compiler_params=_params("arbitrary"), name="in_odd",
    )(*args)


def _conv_b_kernel(u_ref, w_ref, b_ref, g_ref, beta_ref, o_ref, pad_ref, *, seq_len):
    halo = 16
    chunk = 128
    zeros = jnp.zeros((halo, B_WIDTH), F32)
    pad_ref[0:halo, :] = zeros
    pad_ref[halo + seq_len:2 * halo + seq_len, :] = zeros
    pad_ref[halo:halo + seq_len, :] = u_ref[...]
    off = halo - B_CONV_W // 2
    for r in range(seq_len // chunk):
        cols = []
        for c in range(B_WIDTH // LANES):
            lanes = slice(c * LANES, (c + 1) * LANES)
            acc = jnp.zeros((chunk, LANES), F32) + b_ref[:, lanes]
            for k in range(B_CONV_W):
                start = r * chunk + off + k
                acc = acc + pad_ref[start:start + chunk, lanes] * w_ref[k:k + 1, lanes]
            cols.append(acc)
        y = _layer_norm(jnp.concatenate(cols, axis=-1), g_ref[...], beta_ref[...])
        o_ref[r * chunk:(r + 1) * chunk, :] = (y * _sigmoid(y)).astype(BF16)


def _conv_b(u, w, b, g, beta, *, seq_len):
    rows = u.shape[0]
    full = lambda a: pl.BlockSpec(a.shape, lambda i: (0,) * a.ndim)
    return pl.pallas_call(
        functools.partial(_conv_b_kernel, seq_len=seq_len),
        grid=(rows // seq_len,),
        in_specs=[pl.BlockSpec((seq_len, B_WIDTH), lambda i: (i, 0)), full(w), full(b), full(g), full(beta)],
        out_specs=pl.BlockSpec((seq_len, B_WIDTH), lambda i: (i, 0)),
        out_shape=jax.ShapeDtypeStruct((rows, B_WIDTH), BF16),
        scratch_shapes=[pltpu.VMEM((seq_len + 32, B_WIDTH), F32)],
        compiler_params=_params("arbitrary"), name="conv_b",
    )(u, w, b, g, beta)


def _pipelined(n, scores, consume):
    nxt = scores(0)
    outs = []
    for i in range(n):
        cur = nxt
        if i + 1 < n:
            nxt = scores(i + 1)
        outs.append(consume(i, cur))
    return outs


def _half_lanes(x, half):
    lane = lax.broadcasted_iota(jnp.int32, x.shape, 1)
    return jnp.where((lane < HEAD_DIM) == (half == 0), x, jnp.zeros_like(x))


def _softmax_pv_t(s, vts):
    m = s[0].max(axis=0, keepdims=True)
    for x in s[1:]:
        m = jnp.maximum(m, x.max(axis=0, keepdims=True))
    p = [jnp.exp(x - m) for x in s]
    l = p[0].sum(axis=0, keepdims=True)
    for x in p[1:]:
        l = l + x.sum(axis=0, keepdims=True)
    o = _dot(vts[0], p[0].astype(BF16))
    for vt, pp in zip(vts[1:], p[1:]):
        o = o + _dot(vt, pp.astype(BF16))
    return o * (1.0 / l)


def _attn_even_kernel(*refs, latent):
    if latent:
        q_ref, k_ref, vt_ref, ck_ref, cv_ref, o_ref = refs
        ck = ck_ref[...]
        ctx_keys = [ck.astype(BF16), pltpu.roll(ck, HEAD_DIM, 1).astype(BF16)]
        ctx_vt = cv_ref[...].T.astype(BF16)
    else:
        q_ref, k_ref, vt_ref, o_ref = refs
    group = A_HEADS // A_KV_HEADS

    def scores(h):
        j, half = h // group, h % 2
        lo = (h // 2) * LANES
        q = _half_lanes(q_ref[:, lo:lo + LANES], half)
        order = 0 if half == j else 1
        s = [_dot_nt(k_ref[:, order * LANES:(order + 1) * LANES], q)]
        if latent:
            s.append(_dot_nt(ctx_keys[order], q))
        return s

    def consume(h, s):
        rows = slice((h // group) * HEAD_DIM, (h // group + 1) * HEAD_DIM)
        vts = [vt_ref[rows, :]]
        if latent:
            vts.append(ctx_vt[rows, :])
        return _softmax_pv_t(s, vts)

    outs = _pipelined(A_HEADS, scores, consume)
    o_ref[...] = jnp.concatenate(outs, axis=0).T.astype(BF16)


def _attn_even(q, k2, vt, cache, *, seq_len, latent):
    rows = q.shape[0]
    tq = 256
    tps = seq_len // tq
    q_spec = pl.BlockSpec((tq, A_WIDTH), lambda b, i: (b * tps + i, 0))
    in_specs = [q_spec, pl.BlockSpec((seq_len, 2 * A_KV_WIDTH), lambda b, i: (b, 0)),
                pl.BlockSpec((A_KV_WIDTH, seq_len), lambda b, i: (0, b))]
    args = [q, k2, vt]
    if latent:
        past = cache[0].shape[1]
        c_spec = pl.BlockSpec((None, past, A_KV_WIDTH), lambda b, i: (b, 0, 0))
        in_specs += [c_spec, c_spec]
        args += [cache[0].reshape(-1, past, A_KV_WIDTH), cache[1].reshape(-1, past, A_KV_WIDTH)]
    return pl.pallas_call(
        functools.partial(_attn_even_kernel, latent=latent),
        grid=(rows // seq_len, tps), in_specs=in_specs, out_specs=q_spec,
        out_shape=jax.ShapeDtypeStruct((rows, A_WIDTH), BF16),
        compiler_params=_params("arbitrary", "arbitrary"), name="attn_even",
    )(*args)


def _attn_odd_kernel(*refs, latent, lam_init, heads):
    if latent:
        q_ref, k_ref, vt_ref, ck_ref, cv_ref, dl_ref, sg_ref, o_ref = refs
    else:
        q_ref, k_ref, vt_ref, dl_ref, sg_ref, o_ref = refs
    dl = dl_ref[...]
    lam = (jnp.exp(jnp.sum(dl[0:1] * dl[1:2], axis=-1, keepdims=True))
           - jnp.exp(jnp.sum(dl[2:3] * dl[3:4], axis=-1, keepdims=True)) + lam_init)
    def lanes_of(i):
        return slice((i // 2) * C_VDIM, (i // 2 + 1) * C_VDIM)

    def scores(i):
        lanes = lanes_of(i)
        q = _half_lanes(q_ref[:, lanes], i % 2)
        s = [_dot_nt(k_ref[:, lanes], q)]
        if latent:
            s.append(_dot_nt(ck_ref[:, lanes].astype(BF16), q))
        return s

    def consume(i, s):
        lanes = lanes_of(i)
        vts = [vt_ref[lanes, :]]
        if latent:
            vts.append(cv_ref[:, lanes].T.astype(BF16))
        return _softmax_pv_t(s, vts)

    outs = _pipelined(2 * heads, scores, consume)
    for h in range(heads):
        o = outs[2 * h] - lam * outs[2 * h + 1]
        ms = jnp.mean(o * o, axis=0, keepdims=True)
        o = o * lax.rsqrt(ms + EPS) * sg_ref[...] * (1.0 - lam_init)
        o_ref[:, h * C_VDIM:(h + 1) * C_VDIM] = o.T.astype(BF16)


def _attn_odd(q, k, vt, cache, dl, sg, *, seq_len, latent, lam_init):
    rows = q.shape[0]
    tq = 256
    heads = 4 if latent else C_HEADS
    tps = seq_len // tq
    w = heads * C_VDIM
    q_spec = pl.BlockSpec((tq, w), lambda b, h, i: (b * tps + i, h))
    full = lambda a: pl.BlockSpec(a.shape, lambda b, h, i: (0,) * a.ndim)
    in_specs = [q_spec, pl.BlockSpec((seq_len, w), lambda b, h, i: (b, h)),
                pl.BlockSpec((w, seq_len), lambda b, h, i: (h, b))]
    args = [q, k, vt]
    if latent:
        past = cache[0].shape[1]
        c_spec = pl.BlockSpec((None, past, w), lambda b, h, i: (b, 0, h))
        in_specs += [c_spec, c_spec]
        args += [cache[0].reshape(-1, past, C_WIDTH), cache[1].reshape(-1, past, C_WIDTH)]
    in_specs += [full(dl), full(sg)]
    args += [dl, sg]
    return pl.pallas_call(
        functools.partial(_attn_odd_kernel, latent=latent, lam_init=lam_init, heads=heads),
        grid=(rows // seq_len, C_HEADS // heads, tps), in_specs=in_specs, out_specs=q_spec,
        out_shape=jax.ShapeDtypeStruct((rows, C_WIDTH), BF16),
        compiler_params=_params("arbitrary", "arbitrary", "arbitrary"), name="attn_odd",
    )(*args)


def _out_kernel(*refs, n_parts):
    parts = refs[:n_parts]
    w_ref, x_ref, g_ref, lng_ref, lnb_ref, o_ref = refs[n_parts:]
    width = w_ref.shape[0] // n_parts
    mix = _dot(parts[0][...], w_ref[0:width, :])
    for j in range(1, n_parts):
        mix = mix + _dot(parts[j][...], w_ref[j * width:(j + 1) * width, :])
    y = ALPHA * x_ref[...] + g_ref[...] * mix
    o_ref[...] = _layer_norm(y, lng_ref[...], lnb_ref[...])


def _out_proj(parts, w_out, x, mod, layer, ln_g, ln_b, *, seq_len, latent):
    rows = x.shape[0]
    tm = 256
    tps = seq_len // tm
    row_spec = lambda w: pl.BlockSpec((tm, w), lambda i: (i, 0))
    full = lambda a: pl.BlockSpec(a.shape, lambda i: (0,) * a.ndim)
    in_specs = [row_spec(p.shape[1]) for p in parts]
    in_specs += [full(w_out), row_spec(D_MODEL), _mod_spec(layer, 2, tps, latent), full(ln_g), full(ln_b)]
    return pl.pallas_call(
        functools.partial(_out_kernel, n_parts=len(parts)),
        grid=(rows // tm,), in_specs=in_specs, out_specs=row_spec(D_MODEL),
        out_shape=jax.ShapeDtypeStruct((rows, D_MODEL), F32),
        compiler_params=_params("arbitrary"), name="out_proj_ln",
    )(*parts, w_out, x, mod, ln_g, ln_b)


def _neighbour_rows(u):
    n = u.shape[0]
    sub = lax.broadcasted_iota(jnp.int32, (8, u.shape[1]), 0)
    prev = pltpu.roll(u, 1, 0)
    nxt = pltpu.roll(u, n - 1, 0)
    prev = jnp.concatenate([jnp.where(sub == 0, 0.0, prev[0:8]), prev[8:]], axis=0)
    nxt = jnp.concatenate([nxt[:n - 8], jnp.where(sub == 7, 0.0, nxt[n - 8:])], axis=0)
    return prev, nxt


def _ffn_kernel(x_ref, sh_ref, sc_ref, g_ref, lng_ref, lnb_ref, wu_ref, cw_ref, cb_ref, wd_ref, o_ref, a_ref):
    h = (x_ref[...] * (1.0 + sc_ref[...]) + sh_ref[...]).astype(BF16)

    def conv(u, j):
        prev, nxt = _neighbour_rows(u)
        return prev * cw_ref[j, 0:1, :] + u * cw_ref[j, 1:2, :] + nxt * cw_ref[j, 2:3, :] + cb_ref[j]

    for j in range(FF_CHUNKS):
        gate = conv(_dot(h, wu_ref[j]), j)
        val = conv(_dot(h, wu_ref[FF_CHUNKS + j]), FF_CHUNKS + j)
        a_ref[:, j * FF_CHUNK:(j + 1) * FF_CHUNK] = (gate * _sigmoid(gate) * val).astype(BF16)
    y = ALPHA * x_ref[...] + g_ref[...] * _dot(a_ref[...], wd_ref[...])
    o_ref[...] = _layer_norm(y, lng_ref[...], lnb_ref[...])


def _ffn(x, mod, layer, ln_g, ln_b, w_up, cw, cb, w_down, *, seq_len, latent):
    rows = x.shape[0]
    row_spec = pl.BlockSpec((seq_len, D_MODEL), lambda i: (i, 0))
    const = lambda a: pl.BlockSpec(a.shape, lambda i: (0,) * a.ndim, pipeline_mode=pl.Buffered(1))
    in_specs = [row_spec, _mod_spec(layer, 3, 1, latent), _mod_spec(layer, 4, 1, latent),
                _mod_spec(layer, 5, 1, latent), const(ln_g), const(ln_b),
                const(w_up), const(cw), const(cb), const(w_down)]
    return pl.pallas_call(
        _ffn_kernel,
        grid=(rows // seq_len,), in_specs=in_specs, out_specs=row_spec,
        out_shape=jax.ShapeDtypeStruct((rows, D_MODEL), F32),
        scratch_shapes=[pltpu.VMEM((seq_len, D_FF), BF16)],
        compiler_params=_params("arbitrary"), name="conv_ffn",
    )(x, mod, mod, mod, ln_g, ln_b, w_up, cw, cb, w_down)


def _rope_lane_tables(n):
    t = jnp.arange(n)
    half = HEAD_DIM // 2
    inv = 1.0 / (ROPE_THETA ** (jnp.arange(0, half, 2, dtype=F32) / half))
    ang_r = (t // GRID_W).astype(F32)[:, None] * inv
    ang_c = (t % GRID_W).astype(F32)[:, None] * inv
    cos = jnp.concatenate([jnp.cos(ang_r)] * 2 + [jnp.cos(ang_c)] * 2, axis=-1)
    sin = jnp.concatenate([-jnp.sin(ang_r), jnp.sin(ang_r), -jnp.sin(ang_c), jnp.sin(ang_c)], axis=-1)
    reps = LANES // HEAD_DIM
    return jnp.tile(cos, (1, reps)), jnp.tile(sin, (1, reps))


def kernel(x_prompt, x_sample, cache_a_k_l0, cache_a_v_l0, cache_c_k_l1, cache_c_v_l1, cache_a_k_l2,
           cache_a_v_l2, cache_c_k_l3, cache_c_v_l3, c, c_ctx, w_ada, b_ada, ln_g, ln_b, w_in_even,
           q_norm_g, k_norm_g, b_conv_w, b_conv_b, b_norm_g, b_norm_b, w_out_even, w_in_odd,
           diff_lambda, subln_g, w_out_odd, w_up, ffn_conv_w, ffn_conv_b, w_down):
    batch, seq, d = x_prompt.shape
    dec_batch, dec_seq, _ = x_sample.shape
    assert d == D_MODEL and 1 + dec_batch <= MOD_ROWS
    caches = [(cache_a_k_l0, cache_a_v_l0), (cache_c_k_l1, cache_c_v_l1),
              (cache_a_k_l2, cache_a_v_l2), (cache_c_k_l3, cache_c_v_l3)]

    cond = jnp.zeros((MOD_ROWS, d), F32).at[0].set(c_ctx).at[1:1 + dec_batch].set(c)
    mod = _modulation(cond, w_ada, b_ada).reshape(DEPTH * MOD_ROWS * 6, 1, d)
    rope = _rope_lane_tables(dec_seq)

    w_in_even_b = w_in_even.astype(BF16)
    w_out_even_b = w_out_even.astype(BF16)
    w_in_odd_b = w_in_odd.astype(BF16)
    w_out_odd_b = w_out_odd.astype(BF16)
    w_up_b = w_up.astype(BF16).reshape(DEPTH, d, 2 * FF_CHUNKS, FF_CHUNK).transpose(0, 2, 1, 3)
    w_down_b = w_down.astype(BF16)
    cw = ffn_conv_w.reshape(DEPTH, 3, 2 * FF_CHUNKS, FF_CHUNK).transpose(0, 2, 1, 3)
    cb = ffn_conv_b.reshape(DEPTH, 2 * FF_CHUNKS, 1, FF_CHUNK)

    streams = [
        [x_prompt.reshape(batch * seq, d), seq, False],
        [x_sample.reshape(dec_batch * dec_seq, d), dec_seq, True],
    ]
    ctx_state = []
    for layer in range(DEPTH):
        lng = ln_g[layer].reshape(2, 1, d)
        lnb = ln_b[layer].reshape(2, 1, d)
        for stream in streams:
            x, s, latent = stream
            kw = dict(seq_len=s, latent=latent)
            if layer % 2 == 0:
                e = layer // 2
                qg = jnp.tile(q_norm_g[e], LANES // HEAD_DIM)[None]
                kg = jnp.tile(k_norm_g[e], LANES // HEAD_DIM)[None]
                res = _in_even(x, mod, layer, w_in_even_b[e], qg, kg, rope, **kw)
                q, k, v, u = res[:4]
                if not latent:
                    ctx_state.append((res[4].reshape(batch, seq, A_KV_HEADS, HEAD_DIM),
                                      res[5].reshape(batch, seq, A_KV_HEADS, HEAD_DIM)))
                attn = _attn_even(q, k, v, caches[layer], **kw)
                conv = _conv_b(u, b_conv_w[e], b_conv_b[e][None], b_norm_g[e][None], b_norm_b[e][None],
                               seq_len=s)
                parts, w_out = [attn, conv], w_out_even_b[e]
            else:
                o = layer // 2
                res = _in_odd(x, mod, layer, w_in_odd_b[o], rope, **kw)
                q, k, v = res[:3]
                if not latent:
                    ctx_state.append((res[3].reshape(batch, seq, C_HEADS, C_VDIM),
                                      res[4].reshape(batch, seq, C_HEADS, C_VDIM)))
                lam_init = 0.8 - 0.6 * math.exp(-0.3 * layer)
                attn = _attn_odd(q, k, v, caches[layer], diff_lambda[o], subln_g[o][:, None],
                                 lam_init=lam_init, **kw)
                parts, w_out = [attn], w_out_odd_b[o]
            x = _out_proj(parts, w_out, x, mod, layer, lng[0], lnb[0], **kw)
            x = _ffn(x, mod, layer, lng[1], lnb[1], w_up_b[layer], cw[layer], cb[layer], w_down_b[layer], **kw)
            stream[0] = x
    y_prompt = streams[0][0].reshape(batch, seq, d)
    y_sample = streams[1][0].reshape(dec_batch, dec_seq, d)
    (a_k0, a_v0), (c_k1, c_v1), (a_k2, a_v2), (c_k3, c_v3) = ctx_state
    return (y_prompt, y_sample, a_k0, a_v0, c_k1, c_v1, a_k2, a_v2, c_k3, c_v3)
```

```python
import functools
import math

import jax
import jax.numpy as jnp
from jax import lax
from jax.experimental import pallas as pl
from jax.experimental.pallas import tpu as pltpu

F32 = jnp.float32
BF16 = jnp.bfloat16

D_MODEL = 1024
DEPTH = 4
GRID_W = 64
HEAD_DIM = 64
A_HEADS = 8
A_KV_HEADS = 2
A_WIDTH = A_HEADS * HEAD_DIM
A_KV_WIDTH = A_KV_HEADS * HEAD_DIM
B_WIDTH = D_MODEL - A_WIDTH
B_CONV_W = 31
C_HEADS = D_MODEL // (2 * HEAD_DIM)
C_VDIM = 2 * HEAD_DIM
C_WIDTH = C_HEADS * C_VDIM
D_FF = 2816
ROPE_THETA = 10000.0
EPS = 1e-6
ATTN_SCALE = HEAD_DIM ** -0.5
LOG2E = math.log2(math.e)
Q_SCALE = ATTN_SCALE * LOG2E
ALPHA = (2 * DEPTH) ** 0.25
EVEN_IN = A_WIDTH + 2 * A_KV_WIDTH + 2 * B_WIDTH
ODD_IN = 3 * C_WIDTH

LANES = 128
SUBLANES = 8
BF16_ROWS = 16
MXU_DIM = 256
FF_CHUNK = MXU_DIM
FF_CHUNKS = D_FF // FF_CHUNK
MOD_ROWS = 16
TOKEN_TILE = 256
VMEM_LIMIT = 56 * 1024 * 1024


def _params(*sem):
    return pltpu.CompilerParams(dimension_semantics=sem, vmem_limit_bytes=VMEM_LIMIT)


def _dot(a, b):
    return jnp.dot(a, b, preferred_element_type=F32)


def _dot_nt(a, b):
    return lax.dot_general(a, b, (((1,), (1,)), ((), ())), preferred_element_type=F32)


def _sigmoid(x):
    return 1.0 / (1.0 + jnp.exp(-x))


def _layer_norm(y, g, b):
    mu = jnp.mean(y, axis=-1, keepdims=True)
    d = y - mu
    var = jnp.mean(d * d, axis=-1, keepdims=True)
    return d * lax.rsqrt(var + EPS) * g + b


def _group_mean_sq(x, group):
    r = lax.broadcasted_iota(jnp.int32, (LANES, LANES), 0) // group
    c = lax.broadcasted_iota(jnp.int32, (LANES, LANES), 1) // group
    ones = jnp.where(r == c, 1.0, 0.0).astype(BF16)
    sq = x * x
    hi = sq.astype(BF16)
    lo = (sq - hi.astype(F32)).astype(BF16)
    return (_dot(hi, ones) + _dot(lo, ones)) * (1.0 / group)


def _rope(x, cos, sin):
    lane = lax.broadcasted_iota(jnp.int32, x.shape, 1)
    partner = jnp.where((lane & 16) == 0, pltpu.roll(x, LANES - 16, 1), pltpu.roll(x, 16, 1))
    return x * cos + partner * sin


def _layer_spec(a, layer, single_buffer=False):
    mode = dict(pipeline_mode=pl.Buffered(1)) if single_buffer else {}
    return pl.BlockSpec((None,) + a.shape[1:], lambda *_: (layer,) + (0,) * (a.ndim - 1), **mode)


def _mod_kernel(c_ref, w_ref, b_ref, o_ref):
    c = c_ref[...]
    s = (c * _sigmoid(c)).astype(BF16)
    o_ref[...] = _dot(s, w_ref[...].astype(BF16)) + b_ref[...]


def _modulation(cond, w_ada, b_ada):
    tn = 1536
    n = 6 * D_MODEL
    return pl.pallas_call(
        _mod_kernel,
        grid=(DEPTH, n // tn),
        in_specs=[
            pl.BlockSpec((MOD_ROWS, D_MODEL), lambda l, j: (0, 0)),
            pl.BlockSpec((None, D_MODEL, tn), lambda l, j: (l, 0, j)),
            pl.BlockSpec((None, 1, tn), lambda l, j: (l, 0, j)),
        ],
        out_specs=pl.BlockSpec((None, MOD_ROWS, tn), lambda l, j: (l, 0, j)),
        out_shape=jax.ShapeDtypeStruct((DEPTH, MOD_ROWS, n), F32),
        compiler_params=_params("arbitrary", "arbitrary"), name="adaln_mod",
    )(cond, w_ada, b_ada.reshape(DEPTH, 1, n))


def _mod_spec(layer, which, tiles_per_seq, latent):
    def index(i, *_):
        row = 1 + i // tiles_per_seq if latent else 0
        return ((layer * MOD_ROWS + row) * 6 + which, 0, 0)
    return pl.BlockSpec((None, 1, D_MODEL), index)


def _in_even_kernel(*refs, latent):
    if latent:
        x_ref, sh_ref, sc_ref, w_ref, qg_ref, kg_ref, cos_ref, sin_ref, q_ref, k_ref, v_ref, u_ref = refs
    else:
        x_ref, sh_ref, sc_ref, w_ref, qg_ref, kg_ref, q_ref, k_ref, v_ref, u_ref, ko_ref, vo_ref = refs
    h = (x_ref[...] * (1.0 + sc_ref[...]) + sh_ref[...]).astype(BF16)
    proj = _dot(h, w_ref[...])
    for c in range((A_WIDTH + A_KV_WIDTH) // LANES):
        lo = c * LANES
        x = proj[:, lo:lo + LANES]
        is_q = lo < A_WIDTH
        gain = qg_ref[...] if is_q else kg_ref[...]
        x = x * lax.rsqrt(_group_mean_sq(x, HEAD_DIM) + EPS) * gain
        if not latent and not is_q:
            ko_ref[...] = x
        if latent:
            x = _rope(x, cos_ref[...], sin_ref[...])
        if is_q:
            q_ref[:, lo:lo + LANES] = (x * Q_SCALE).astype(BF16)
        else:
            k_ref[:, 0:LANES] = x.astype(BF16)
            k_ref[:, LANES:2 * LANES] = pltpu.roll(x, HEAD_DIM, 1).astype(BF16)
    v = proj[:, A_WIDTH + A_KV_WIDTH:A_WIDTH + 2 * A_KV_WIDTH]
    v_ref[...] = v.T.astype(BF16)
    if not latent:
        vo_ref[...] = v
    u0 = A_WIDTH + 2 * A_KV_WIDTH
    ua = proj[:, u0:u0 + B_WIDTH]
    ub = proj[:, u0 + B_WIDTH:u0 + 2 * B_WIDTH]
    u_ref[...] = ua * _sigmoid(ub)


def _in_even(x, mod, layer, w_in, qg, kg, rope, *, seq_len, latent):
    rows = x.shape[0]
    tm = TOKEN_TILE
    tps = seq_len // tm
    row_spec = lambda w: pl.BlockSpec((tm, w), lambda i: (i, 0))
    full = lambda a: pl.BlockSpec(a.shape, lambda i: (0,) * a.ndim)
    in_specs = [row_spec(D_MODEL), _mod_spec(layer, 0, tps, latent), _mod_spec(layer, 1, tps, latent),
                _layer_spec(w_in, layer // 2), full(qg), full(kg)]
    args = [x, mod, mod, w_in, qg, kg]
    out_specs = [row_spec(A_WIDTH), row_spec(2 * A_KV_WIDTH),
                 pl.BlockSpec((A_KV_WIDTH, tm), lambda i: (0, i)), row_spec(B_WIDTH)]
    out_shape = [jax.ShapeDtypeStruct((rows, A_WIDTH), BF16), jax.ShapeDtypeStruct((rows, 2 * A_KV_WIDTH), BF16),
                 jax.ShapeDtypeStruct((A_KV_WIDTH, rows), BF16), jax.ShapeDtypeStruct((rows, B_WIDTH), F32)]
    if latent:
        pos_spec = pl.BlockSpec((tm, LANES), lambda i: (i % tps, 0))
        in_specs += [pos_spec, pos_spec]
        args += list(rope)
    else:
        out_specs += [row_spec(A_KV_WIDTH), row_spec(A_KV_WIDTH)]
        out_shape += [jax.ShapeDtypeStruct((rows, A_KV_WIDTH), F32)] * 2
    return pl.pallas_call(
        functools.partial(_in_even_kernel, latent=latent),
        grid=(rows // tm,), in_specs=in_specs, out_specs=out_specs, out_shape=out_shape,
        compiler_params=_params("arbitrary"), name="in_even",
    )(*args)


def _in_odd_kernel(*refs, latent):
    if latent:
        x_ref, sh_ref, sc_ref, w_ref, cos_ref, sin_ref, q_ref, k_ref, v_ref = refs
    else:
        x_ref, sh_ref, sc_ref, w_ref, q_ref, k_ref, v_ref, ko_ref, vo_ref = refs
    h = (x_ref[...] * (1.0 + sc_ref[...]) + sh_ref[...]).astype(BF16)
    proj = _dot(h, w_ref[...])
    for c in range(2 * C_WIDTH // LANES):
        lo = c * LANES
        x = proj[:, lo:lo + LANES]
        if latent:
            x = _rope(x, cos_ref[...], sin_ref[...])
        if lo < C_WIDTH:
            q_ref[:, lo:lo + LANES] = (x * Q_SCALE).astype(BF16)
        else:
            k_ref[:, lo - C_WIDTH:lo - C_WIDTH + LANES] = x.astype(BF16)
    v = proj[:, 2 * C_WIDTH:]
    v_ref[...] = v.T.astype(BF16)
    if not latent:
        ko_ref[...] = proj[:, C_WIDTH:2 * C_WIDTH]
        vo_ref[...] = v


def _in_odd(x, mod, layer, w_in, rope, *, seq_len, latent):
    rows = x.shape[0]
    tm = TOKEN_TILE
    tps = seq_len // tm
    row_spec = lambda w: pl.BlockSpec((tm, w), lambda i: (i, 0))
    in_specs = [row_spec(D_MODEL), _mod_spec(layer, 0, tps, latent), _mod_spec(layer, 1, tps, latent),
                _layer_spec(w_in, layer // 2)]
    args = [x, mod, mod, w_in]
    out_specs = [row_spec(C_WIDTH)] * 2 + [pl.BlockSpec((C_WIDTH, tm), lambda i: (0, i))]
    out_shape = [jax.ShapeDtypeStruct((rows, C_WIDTH), BF16)] * 2 + [jax.ShapeDtypeStruct((C_WIDTH, rows), BF16)]
    if latent:
        pos_spec = pl.BlockSpec((tm, LANES), lambda i: (i % tps, 0))
        in_specs += [pos_spec, pos_spec]
        args += list(rope)
    else:
        out_specs += [row_spec(C_WIDTH)] * 2
        out_shape += [jax.ShapeDtypeStruct((rows, C_WIDTH), F32)] * 2
    return pl.pallas_call(
        functools.partial(_in_odd_kernel, latent=latent),
        grid=(rows // tm,), in_specs=in_specs, out_specs=out_specs, out_shape=out_shape,
        compiler_params=_params("arbitrary"), name="in_odd",
    )(*args)


CONV_HALO = 16


def _conv_b_kernel(u_ref, w_ref, b_ref, g_ref, beta_ref, o_ref, pad_ref, *, seq_len):
    chunk = 64
    zeros = jnp.zeros((CONV_HALO, B_WIDTH), F32)
    pad_ref[0:CONV_HALO, :] = zeros
    pad_ref[CONV_HALO + seq_len:2 * CONV_HALO + seq_len, :] = zeros
    pad_ref[CONV_HALO:CONV_HALO + seq_len, :] = u_ref[...]
    centre = B_CONV_W // 2
    for r in range(seq_len // chunk):
        base = CONV_HALO + r * chunk
        cols = []
        for c in range(B_WIDTH // LANES):
            lanes = slice(c * LANES, (c + 1) * LANES)
            acc = None
            for b in range(SUBLANES):
                part = None
                for a in range(-2, 2):
                    k = SUBLANES * a + b + centre
                    if not 0 <= k < B_CONV_W:
                        continue
                    start = base + SUBLANES * a
                    term = pad_ref[start:start + chunk + SUBLANES, lanes] * w_ref[k:k + 1, lanes]
                    part = term if part is None else part + term
                part = part[b:b + chunk]
                acc = part if acc is None else acc + part
            cols.append(acc + b_ref[:, lanes])
        y = _layer_norm(jnp.concatenate(cols, axis=-1), g_ref[...], beta_ref[...])
        o_ref[r * chunk:(r + 1) * chunk, :] = (y * _sigmoid(y)).astype(BF16)


def _conv_b(u, w, b, g, beta, layer, *, seq_len):
    rows = u.shape[0]
    e = layer // 2
    return pl.pallas_call(
        functools.partial(_conv_b_kernel, seq_len=seq_len),
        grid=(rows // seq_len,),
        in_specs=[pl.BlockSpec((seq_len, B_WIDTH), lambda i: (i, 0)), _layer_spec(w, e),
                  _layer_spec(b, e), _layer_spec(g, e), _layer_spec(beta, e)],
        out_specs=pl.BlockSpec((seq_len, B_WIDTH), lambda i: (i, 0)),
        out_shape=jax.ShapeDtypeStruct((rows, B_WIDTH), BF16),
        scratch_shapes=[pltpu.VMEM((seq_len + 2 * CONV_HALO, B_WIDTH), F32)],
        compiler_params=_params("arbitrary"), name="conv_b",
    )(u, w, b, g, beta)


SCORE_LOOKAHEAD = 2


def _pipelined(n, scores, consume):
    pending = [scores(i) for i in range(min(SCORE_LOOKAHEAD, n))]
    outs = []
    for i in range(n):
        cur = pending.pop(0)
        if i + SCORE_LOOKAHEAD < n:
            pending.append(scores(i + SCORE_LOOKAHEAD))
        outs.append(consume(i, cur))
    return outs


def _half_lanes(x, half):
    lane = lax.broadcasted_iota(jnp.int32, x.shape, 1)
    return jnp.where((lane < HEAD_DIM) == (half == 0), x, jnp.zeros_like(x))


def _softmax_pv_t(s, vt_ones):
    v_dim = vt_ones.shape[0] - BF16_ROWS
    p = jnp.exp2(s - s.max(axis=0, keepdims=True)).astype(BF16)
    o = _dot(vt_ones, p)
    return o[0:v_dim] * (1.0 / o[v_dim:v_dim + 1])


A_VT_ROWS = HEAD_DIM + BF16_ROWS
C_VT_ROWS = C_VDIM + BF16_ROWS


def _attn_even_kernel(*refs, latent, seq_len):
    if latent:
        q_ref, k_ref, vt_ref, ck_ref, cv_ref, o_ref, kall_ref, vtall_ref = refs
    else:
        q_ref, k_ref, vt_ref, o_ref, vtall_ref = refs
        kall_ref = k_ref
    n_keys = vtall_ref.shape[1]

    @pl.when(pl.program_id(1) == 0)
    def _():
        if latent:
            kall_ref[0:seq_len, :] = k_ref[...]
            ck = ck_ref[...]
            kall_ref[seq_len:, 0:LANES] = ck.astype(BF16)
            kall_ref[seq_len:, LANES:2 * LANES] = pltpu.roll(ck, HEAD_DIM, 1).astype(BF16)
            cvt = cv_ref[...].T
        for j in range(A_KV_HEADS):
            r0 = j * A_VT_ROWS
            vtall_ref[r0:r0 + HEAD_DIM, 0:seq_len] = vt_ref[j * HEAD_DIM:(j + 1) * HEAD_DIM, :]
            if latent:
                vtall_ref[r0:r0 + HEAD_DIM, seq_len:] = cvt[j * HEAD_DIM:(j + 1) * HEAD_DIM].astype(BF16)
            vtall_ref[r0 + HEAD_DIM:r0 + A_VT_ROWS, :] = jnp.ones((BF16_ROWS, n_keys), BF16)

    group = A_HEADS // A_KV_HEADS

    def scores(h):
        j, half = h // group, h % 2
        lo = (h // 2) * LANES
        q = _half_lanes(q_ref[:, lo:lo + LANES], half)
        order = 0 if half == j else 1
        return _dot_nt(kall_ref[:, order * LANES:(order + 1) * LANES], q)

    def consume(h, s):
        r0 = (h // group) * A_VT_ROWS
        return _softmax_pv_t(s, vtall_ref[r0:r0 + A_VT_ROWS, :])

    outs = _pipelined(A_HEADS, scores, consume)
    o_ref[...] = jnp.concatenate(outs, axis=0).T.astype(BF16)


def _attn_even(q, k2, vt, cache, *, seq_len, latent):
    rows = q.shape[0]
    tq = TOKEN_TILE
    tps = seq_len // tq
    q_spec = pl.BlockSpec((tq, A_WIDTH), lambda b, i: (b * tps + i, 0))
    in_specs = [q_spec, pl.BlockSpec((seq_len, 2 * A_KV_WIDTH), lambda b, i: (b, 0)),
                pl.BlockSpec((A_KV_WIDTH, seq_len), lambda b, i: (0, b))]
    args = [q, k2, vt]
    n_keys = seq_len
    scratch = []
    if latent:
        past = cache[0].shape[1]
        n_keys += past
        c_spec = pl.BlockSpec((None, past, A_KV_WIDTH), lambda b, i: (b, 0, 0))
        in_specs += [c_spec, c_spec]
        args += [cache[0].reshape(-1, past, A_KV_WIDTH), cache[1].reshape(-1, past, A_KV_WIDTH)]
        scratch.append(pltpu.VMEM((n_keys, 2 * A_KV_WIDTH), BF16))
    scratch.append(pltpu.VMEM((A_KV_HEADS * A_VT_ROWS, n_keys), BF16))
    return pl.pallas_call(
        functools.partial(_attn_even_kernel, latent=latent, seq_len=seq_len),
        grid=(rows // seq_len, tps), in_specs=in_specs, out_specs=q_spec,
        out_shape=jax.ShapeDtypeStruct((rows, A_WIDTH), BF16), scratch_shapes=scratch,
        compiler_params=_params("arbitrary", "arbitrary"), name="attn_even",
    )(*args)


def _attn_odd_kernel(*refs, latent, lam_init, seq_len):
    if latent:
        q_ref, k_ref, vt_ref, ck_ref, cv_ref, dl_ref, sg_ref, o_ref, kall_ref, vtall_ref = refs
    else:
        q_ref, k_ref, vt_ref, dl_ref, sg_ref, o_ref, vtall_ref = refs
        kall_ref = k_ref
    n_keys = vtall_ref.shape[1]
    past = n_keys - seq_len

    @pl.when(pl.program_id(1) == 0)
    def _():
        if latent:
            kall_ref[0:seq_len, :] = k_ref[...]
        for h in range(C_HEADS):
            lanes = slice(h * C_VDIM, (h + 1) * C_VDIM)
            r0 = h * C_VT_ROWS
            vtall_ref[r0:r0 + C_VDIM, 0:seq_len] = vt_ref[lanes, :]
            if latent:
                kall_ref[seq_len:, lanes] = ck_ref[pl.ds(h, past, stride=C_HEADS), :].astype(BF16)
                vtall_ref[r0:r0 + C_VDIM, seq_len:] = cv_ref[pl.ds(h, past, stride=C_HEADS), :].T.astype(BF16)
            vtall_ref[r0 + C_VDIM:r0 + C_VT_ROWS, :] = jnp.ones((BF16_ROWS, n_keys), BF16)

    dl = dl_ref[...]
    lam = (jnp.exp(jnp.sum(dl[0:1] * dl[1:2], axis=-1, keepdims=True))
           - jnp.exp(jnp.sum(dl[2:3] * dl[3:4], axis=-1, keepdims=True)) + lam_init)

    def scores(i):
        lanes = slice((i // 2) * C_VDIM, (i // 2 + 1) * C_VDIM)
        return _dot_nt(kall_ref[:, lanes], _half_lanes(q_ref[:, lanes], i % 2))

    def consume(i, s):
        r0 = (i // 2) * C_VT_ROWS
        return _softmax_pv_t(s, vtall_ref[r0:r0 + C_VT_ROWS, :])

    outs = _pipelined(2 * C_HEADS, scores, consume)
    for h in range(C_HEADS):
        o = outs[2 * h] - lam * outs[2 * h + 1]
        ms = jnp.mean(o * o, axis=0, keepdims=True)
        o = o * lax.rsqrt(ms + EPS) * sg_ref[...] * (1.0 - lam_init)
        o_ref[:, h * C_VDIM:(h + 1) * C_VDIM] = o.T.astype(BF16)


def _attn_odd(q, k, vt, cache, dl, sg, layer, *, seq_len, latent):
    rows = q.shape[0]
    tq = TOKEN_TILE
    tps = seq_len // tq
    lam_init = 0.8 - 0.6 * math.exp(-0.3 * layer)
    q_spec = pl.BlockSpec((tq, C_WIDTH), lambda b, i: (b * tps + i, 0))
    in_specs = [q_spec, pl.BlockSpec((seq_len, C_WIDTH), lambda b, i: (b, 0)),
                pl.BlockSpec((C_WIDTH, seq_len), lambda b, i: (0, b))]
    args = [q, k, vt]
    n_keys = seq_len
    scratch = []
    if latent:
        past = cache[0].shape[1]
        n_keys += past
        c_spec = pl.BlockSpec((None, past * C_HEADS, C_VDIM), lambda b, i: (b, 0, 0))
        in_specs += [c_spec, c_spec]
        args += [cache[0].reshape(-1, past * C_HEADS, C_VDIM), cache[1].reshape(-1, past * C_HEADS, C_VDIM)]
        scratch.append(pltpu.VMEM((n_keys, C_WIDTH), BF16))
    scratch.append(pltpu.VMEM((C_HEADS * C_VT_ROWS, n_keys), BF16))
    in_specs += [_layer_spec(dl, layer // 2), _layer_spec(sg, layer // 2)]
    args += [dl, sg]
    return pl.pallas_call(
        functools.partial(_attn_odd_kernel, latent=latent, lam_init=lam_init, seq_len=seq_len),
        grid=(rows // seq_len, tps), in_specs=in_specs, out_specs=q_spec,
        out_shape=jax.ShapeDtypeStruct((rows, C_WIDTH), BF16), scratch_shapes=scratch,
        compiler_params=_params("arbitrary", "arbitrary"), name="attn_odd",
    )(*args)


def _neighbour_rows(u):
    n = u.shape[0]
    sub = lax.broadcasted_iota(jnp.int32, (SUBLANES, u.shape[1]), 0)
    prev = pltpu.roll(u, 1, 0)
    nxt = pltpu.roll(u, n - 1, 0)
    prev = jnp.concatenate([jnp.where(sub == 0, 0.0, prev[0:SUBLANES]), prev[SUBLANES:]], axis=0)
    nxt = jnp.concatenate([nxt[:n - SUBLANES], jnp.where(sub == SUBLANES - 1, 0.0, nxt[n - SUBLANES:])], axis=0)
    return prev, nxt


def _mix_ffn_kernel(*refs, n_parts):
    parts = refs[:n_parts]
    (wo_ref, x_ref, g1_ref, sh_ref, sc_ref, g2_ref, lng_ref, lnb_ref, wu_ref, cw_ref, cb_ref, wd_ref,
     o_ref, x1_ref, a_ref) = refs[n_parts:]
    width = wo_ref.shape[0] // n_parts
    mix = _dot(parts[0][...], wo_ref[0:width, :])
    for j in range(1, n_parts):
        mix = mix + _dot(parts[j][...], wo_ref[j * width:(j + 1) * width, :])
    x1_ref[...] = _layer_norm(ALPHA * x_ref[...] + g1_ref[...] * mix, lng_ref[0:1, :], lnb_ref[0:1, :])
    h = (x1_ref[...] * (1.0 + sc_ref[...]) + sh_ref[...]).astype(BF16)

    def conv(u, lo):
        cols = slice(lo, lo + FF_CHUNK)
        prev, nxt = _neighbour_rows(u)
        return prev * cw_ref[0:1, cols] + u * cw_ref[1:2, cols] + nxt * cw_ref[2:3, cols] + cb_ref[:, cols]

    for j in range(FF_CHUNKS):
        lo = j * FF_CHUNK
        gate = conv(_dot(h, wu_ref[:, lo:lo + FF_CHUNK]), lo)
        val = conv(_dot(h, wu_ref[:, D_FF + lo:D_FF + lo + FF_CHUNK]), D_FF + lo)
        a_ref[:, lo:lo + FF_CHUNK] = (gate * _sigmoid(gate) * val).astype(BF16)
    y = ALPHA * x1_ref[...] + g2_ref[...] * _dot(a_ref[...], wd_ref[...])
    o_ref[...] = _layer_norm(y, lng_ref[1:2, :], lnb_ref[1:2, :])


def _mix_ffn(parts, w_out, x, mod, layer, ln_g, ln_b, w_up, cw, cb, w_down, *, seq_len, latent):
    rows = x.shape[0]
    row_spec = lambda w: pl.BlockSpec((seq_len, w), lambda i: (i, 0))
    res = lambda a, l: _layer_spec(a, l, single_buffer=True)
    in_specs = [row_spec(p.shape[1]) for p in parts]
    in_specs += [res(w_out, layer // 2), row_spec(D_MODEL), _mod_spec(layer, 2, 1, latent),
                 _mod_spec(layer, 3, 1, latent), _mod_spec(layer, 4, 1, latent), _mod_spec(layer, 5, 1, latent),
                 res(ln_g, layer), res(ln_b, layer), res(w_up, layer), res(cw, layer), res(cb, layer),
                 res(w_down, layer)]
    return pl.pallas_call(
        functools.partial(_mix_ffn_kernel, n_parts=len(parts)),
        grid=(rows // seq_len,), in_specs=in_specs, out_specs=row_spec(D_MODEL),
        out_shape=jax.ShapeDtypeStruct((rows, D_MODEL), F32),
        scratch_shapes=[pltpu.VMEM((seq_len, D_MODEL), F32), pltpu.VMEM((seq_len, D_FF), BF16)],
        compiler_params=_params("arbitrary"), name="mix_ffn",
    )(*parts, w_out, x, mod, mod, mod, mod, ln_g, ln_b, w_up, cw, cb, w_down)


def _rope_lane_tables(n):
    t = jnp.arange(n)
    half = HEAD_DIM // 2
    inv = 1.0 / (ROPE_THETA ** (jnp.arange(0, half, 2, dtype=F32) / half))
    ang_r = (t // GRID_W).astype(F32)[:, None] * inv
    ang_c = (t % GRID_W).astype(F32)[:, None] * inv
    cos = jnp.concatenate([jnp.cos(ang_r)] * 2 + [jnp.cos(ang_c)] * 2, axis=-1)
    sin = jnp.concatenate([-jnp.sin(ang_r), jnp.sin(ang_r), -jnp.sin(ang_c), jnp.sin(ang_c)], axis=-1)
    reps = LANES // HEAD_DIM
    return jnp.tile(cos, (1, reps)), jnp.tile(sin, (1, reps))


def kernel(x_prompt, x_sample, cache_a_k_l0, cache_a_v_l0, cache_c_k_l1, cache_c_v_l1, cache_a_k_l2,
           cache_a_v_l2, cache_c_k_l3, cache_c_v_l3, c, c_ctx, w_ada, b_ada, ln_g, ln_b, w_in_even,
           q_norm_g, k_norm_g, b_conv_w, b_conv_b, b_norm_g, b_norm_b, w_out_even, w_in_odd,
           diff_lambda, subln_g, w_out_odd, w_up, ffn_conv_w, ffn_conv_b, w_down):
    batch, seq, d = x_prompt.shape
    dec_batch, dec_seq, _ = x_sample.shape
    assert d == D_MODEL and 1 + dec_batch <= MOD_ROWS
    caches = [(cache_a_k_l0, cache_a_v_l0), (cache_c_k_l1, cache_c_v_l1),
              (cache_a_k_l2, cache_a_v_l2), (cache_c_k_l3, cache_c_v_l3)]

    cond = jnp.zeros((MOD_ROWS, d), F32).at[0].set(c_ctx).at[1:1 + dec_batch].set(c)
    mod = _modulation(cond, w_ada, b_ada).reshape(DEPTH * MOD_ROWS * 6, 1, d)
    rope = _rope_lane_tables(dec_seq)

    w_in_even_b = w_in_even.astype(BF16)
    w_out_even_b = w_out_even.astype(BF16)
    w_in_odd_b = w_in_odd.astype(BF16)
    w_out_odd_b = w_out_odd.astype(BF16)
    w_up_b = w_up.astype(BF16)
    w_down_b = w_down.astype(BF16)
    cb = ffn_conv_b[:, None, :]
    qg = jnp.tile(q_norm_g, (1, LANES // HEAD_DIM))
    kg = jnp.tile(k_norm_g, (1, LANES // HEAD_DIM))
    conv_b = b_conv_b[:, None, :]
    norm_g = b_norm_g[:, None, :]
    norm_b = b_norm_b[:, None, :]
    sub_g = subln_g[:, :, None]

    streams = [
        [x_prompt.reshape(batch * seq, d), seq, False],
        [x_sample.reshape(dec_batch * dec_seq, d), dec_seq, True],
    ]
    ctx_state = []
    for layer in range(DEPTH):
        for stream in streams:
            x, s, latent = stream
            kw = dict(seq_len=s, latent=latent)
            if layer % 2 == 0:
                e = layer // 2
                res = _in_even(x, mod, layer, w_in_even_b, qg[e][None], kg[e][None], rope, **kw)
                q, k2, vt, u = res[:4]
                if not latent:
                    ctx_state.append((res[4].reshape(batch, seq, A_KV_HEADS, HEAD_DIM),
                                      res[5].reshape(batch, seq, A_KV_HEADS, HEAD_DIM)))
                attn = _attn_even(q, k2, vt, caches[layer], **kw)
                conv = _conv_b(u, b_conv_w, conv_b, norm_g, norm_b, layer, seq_len=s)
                parts, w_out = [attn, conv], w_out_even_b
            else:
                res = _in_odd(x, mod, layer, w_in_odd_b, rope, **kw)
                q, k, vt = res[:3]
                if not latent:
                    ctx_state.append((res[3].reshape(batch, seq, C_HEADS, C_VDIM),
                                      res[4].reshape(batch, seq, C_HEADS, C_VDIM)))
                attn = _attn_odd(q, k, vt, caches[layer], diff_lambda, sub_g, layer, **kw)
                parts, w_out = [attn], w_out_odd_b
            stream[0] = _mix_ffn(parts, w_out, x, mod, layer, ln_g, ln_b, w_up_b, ffn_conv_w, cb, w_down_b, **kw)
    y_prompt = streams[0][0].reshape(batch, seq, d)
    y_sample = streams[1][0].reshape(dec_batch, dec_seq, d)
    (a_k0, a_v0), (c_k1, c_v1), (a_k2, a_v2), (c_k3, c_v3) = ctx_state
    return (y_prompt, y_sample, a_k0, a_v0, c_k1, c_v1, a_k2, a_v2, c_k3, c_v3)
```

```python
import functools
import math

import jax
import jax.numpy as jnp
from jax import lax
from jax.experimental import pallas as pl
from jax.experimental.pallas import tpu as pltpu

F32 = jnp.float32
BF16 = jnp.bfloat16

D_MODEL = 1024
DEPTH = 4
GRID_W = 64
HEAD_DIM = 64
A_HEADS = 8
A_KV_HEADS = 2
A_WIDTH = A_HEADS * HEAD_DIM
A_KV_WIDTH = A_KV_HEADS * HEAD_DIM
B_WIDTH = D_MODEL - A_WIDTH
B_CONV_W = 31
C_HEADS = D_MODEL // (2 * HEAD_DIM)
C_VDIM = 2 * HEAD_DIM
C_WIDTH = C_HEADS * C_VDIM
D_FF = 2816
ROPE_THETA = 10000.0
EPS = 1e-6
ATTN_SCALE = HEAD_DIM ** -0.5
LOG2E = math.log2(math.e)
Q_SCALE = ATTN_SCALE * LOG2E
ALPHA = (2 * DEPTH) ** 0.25
EVEN_IN = A_WIDTH + 2 * A_KV_WIDTH + 2 * B_WIDTH
ODD_IN = 3 * C_WIDTH

LANES = 128
SUBLANES = 8
BF16_ROWS = 16
MXU_DIM = 256
FF_CHUNK = MXU_DIM
FF_CHUNKS = D_FF // FF_CHUNK
MOD_ROWS = 16
QUERY_TILE = 256
PROJ_SUB = 256
PROJ_TILE = 512
FFN_CONTEXT_SEQS = 2
VMEM_LIMIT = 56 * 1024 * 1024


def _params(*sem):
    return pltpu.CompilerParams(dimension_semantics=sem, vmem_limit_bytes=VMEM_LIMIT)


def _dot(a, b):
    return jnp.dot(a, b, preferred_element_type=F32)


def _dot_nt(a, b):
    return lax.dot_general(a, b, (((1,), (1,)), ((), ())), preferred_element_type=F32)


def _sigmoid(x):
    return 1.0 / (1.0 + jnp.exp2(x * (-LOG2E)))


def _layer_norm(y, g, b):
    mu = jnp.mean(y, axis=-1, keepdims=True)
    d = y - mu
    var = jnp.mean(d * d, axis=-1, keepdims=True)
    return d * lax.rsqrt(var + EPS) * g + b


def _group_mean_sq(x, group):
    r = lax.broadcasted_iota(jnp.int32, (LANES, LANES), 0) // group
    c = lax.broadcasted_iota(jnp.int32, (LANES, LANES), 1) // group
    ones = jnp.where(r == c, 1.0, 0.0).astype(BF16)
    sq = x * x
    hi = sq.astype(BF16)
    lo = (sq - hi.astype(F32)).astype(BF16)
    return (_dot(hi, ones) + _dot(lo, ones)) * (1.0 / group)


def _rope(x, cos, sin):
    lane = lax.broadcasted_iota(jnp.int32, x.shape, 1)
    partner = jnp.where((lane & 16) == 0, pltpu.roll(x, LANES - 16, 1), pltpu.roll(x, 16, 1))
    return x * cos + partner * sin


def _layer_spec(a, layer, single_buffer=False):
    mode = dict(pipeline_mode=pl.Buffered(1)) if single_buffer else {}
    return pl.BlockSpec((None,) + a.shape[1:], lambda *_: (layer,) + (0,) * (a.ndim - 1), **mode)


def _mod_kernel(c_ref, w_ref, b_ref, o_ref):
    c = c_ref[...]
    s = (c * _sigmoid(c)).astype(BF16)
    o_ref[...] = _dot(s, w_ref[...].astype(BF16)) + b_ref[...]


def _modulation(cond, w_ada, b_ada):
    tn = 1536
    n = 6 * D_MODEL
    return pl.pallas_call(
        _mod_kernel,
        grid=(DEPTH, n // tn),
        in_specs=[
            pl.BlockSpec((MOD_ROWS, D_MODEL), lambda l, j: (0, 0)),
            pl.BlockSpec((None, D_MODEL, tn), lambda l, j: (l, 0, j)),
            pl.BlockSpec((None, 1, tn), lambda l, j: (l, 0, j)),
        ],
        out_specs=pl.BlockSpec((None, MOD_ROWS, tn), lambda l, j: (l, 0, j)),
        out_shape=jax.ShapeDtypeStruct((DEPTH, MOD_ROWS, n), F32),
        compiler_params=_params("arbitrary", "arbitrary"), name="adaln_mod",
    )(cond, w_ada, b_ada.reshape(DEPTH, 1, n))


def _mod_spec(layer, which, tiles_per_seq, latent):
    def index(i, *_):
        row = 1 + i // tiles_per_seq if latent else 0
        return ((layer * MOD_ROWS + row) * 6 + which, 0, 0)
    return pl.BlockSpec((None, 1, D_MODEL), index)


def _sub_tile_projections(x_ref, sh_ref, sc_ref, w_ref):
    h = (x_ref[...] * (1.0 + sc_ref[...]) + sh_ref[...]).astype(BF16)
    subs = [slice(r, r + PROJ_SUB) for r in range(0, h.shape[0], PROJ_SUB)]
    return list(zip(subs, [_dot(h[rows], w_ref[...]) for rows in subs]))

def _in_even_kernel(*refs, latent):
    if latent:
        x_ref, sh_ref, sc_ref, w_ref, qg_ref, kg_ref, cos_ref, sin_ref, q_ref, k_ref, v_ref, u_ref = refs
    else:
        x_ref, sh_ref, sc_ref, w_ref, qg_ref, kg_ref, q_ref, k_ref, v_ref, u_ref, ko_ref, vo_ref = refs
    for rows, proj in _sub_tile_projections(x_ref, sh_ref, sc_ref, w_ref):
        for c in range((A_WIDTH + A_KV_WIDTH) // LANES):
            lo = c * LANES
            x = proj[:, lo:lo + LANES]
            is_q = lo < A_WIDTH
            gain = qg_ref[...] if is_q else kg_ref[...]
            x = x * lax.rsqrt(_group_mean_sq(x, HEAD_DIM) + EPS) * gain
            if not latent and not is_q:
                ko_ref[rows, :] = x
            if latent:
                x = _rope(x, cos_ref[rows, :], sin_ref[rows, :])
            if is_q:
                q_ref[rows, lo:lo + LANES] = (x * Q_SCALE).astype(BF16)
            else:
                k_ref[rows, 0:LANES] = x.astype(BF16)
                k_ref[rows, LANES:2 * LANES] = pltpu.roll(x, HEAD_DIM, 1).astype(BF16)
        v = proj[:, A_WIDTH + A_KV_WIDTH:A_WIDTH + 2 * A_KV_WIDTH]
        v_ref[:, rows] = v.T.astype(BF16)
        if not latent:
            vo_ref[rows, :] = v
        u0 = A_WIDTH + 2 * A_KV_WIDTH
        ua = proj[:, u0:u0 + B_WIDTH]
        ub = proj[:, u0 + B_WIDTH:u0 + 2 * B_WIDTH]
        u_ref[rows, :] = ua * _sigmoid(ub)


def _in_even(x, mod, layer, w_in, qg, kg, rope, *, seq_len, latent):
    rows = x.shape[0]
    tm = PROJ_TILE
    tps = max(seq_len // tm, 1)
    row_spec = lambda w: pl.BlockSpec((tm, w), lambda i: (i, 0))
    full = lambda a: pl.BlockSpec(a.shape, lambda i: (0,) * a.ndim)
    in_specs = [row_spec(D_MODEL), _mod_spec(layer, 0, tps, latent), _mod_spec(layer, 1, tps, latent),
                _layer_spec(w_in, layer // 2), full(qg), full(kg)]
    args = [x, mod, mod, w_in, qg, kg]
    out_specs = [row_spec(A_WIDTH), row_spec(2 * A_KV_WIDTH),
                 pl.BlockSpec((A_KV_WIDTH, tm), lambda i: (0, i)), row_spec(B_WIDTH)]
    out_shape = [jax.ShapeDtypeStruct((rows, A_WIDTH), BF16), jax.ShapeDtypeStruct((rows, 2 * A_KV_WIDTH), BF16),
                 jax.ShapeDtypeStruct((A_KV_WIDTH, rows), BF16), jax.ShapeDtypeStruct((rows, B_WIDTH), F32)]
    if latent:
        pos_spec = pl.BlockSpec((tm, LANES), lambda i: (i % tps, 0))
        in_specs += [pos_spec, pos_spec]
        args += list(rope)
    else:
        out_specs += [row_spec(A_KV_WIDTH), row_spec(A_KV_WIDTH)]
        out_shape += [jax.ShapeDtypeStruct((rows, A_KV_WIDTH), F32)] * 2
    return pl.pallas_call(
        functools.partial(_in_even_kernel, latent=latent),
        grid=(rows // tm,), in_specs=in_specs, out_specs=out_specs, out_shape=out_shape,
        compiler_params=_params("arbitrary"), name="in_even",
    )(*args)


def _in_odd_kernel(*refs, latent):
    if latent:
        x_ref, sh_ref, sc_ref, w_ref, cos_ref, sin_ref, q_ref, k_ref, v_ref = refs
    else:
        x_ref, sh_ref, sc_ref, w_ref, q_ref, k_ref, v_ref, ko_ref, vo_ref = refs
    for rows, proj in _sub_tile_projections(x_ref, sh_ref, sc_ref, w_ref):
        for c in range(2 * C_WIDTH // LANES):
            lo = c * LANES
            x = proj[:, lo:lo + LANES]
            if latent:
                x = _rope(x, cos_ref[rows, :], sin_ref[rows, :])
            if lo < C_WIDTH:
                q_ref[rows, lo:lo + LANES] = (x * Q_SCALE).astype(BF16)
            else:
                k_ref[rows, lo - C_WIDTH:lo - C_WIDTH + LANES] = x.astype(BF16)
        v = proj[:, 2 * C_WIDTH:]
        v_ref[:, rows] = v.T.astype(BF16)
        if not latent:
            ko_ref[rows, :] = proj[:, C_WIDTH:2 * C_WIDTH]
            vo_ref[rows, :] = v


def _in_odd(x, mod, layer, w_in, rope, *, seq_len, latent):
    rows = x.shape[0]
    tm = PROJ_TILE
    tps = max(seq_len // tm, 1)
    row_spec = lambda w: pl.BlockSpec((tm, w), lambda i: (i, 0))
    in_specs = [row_spec(D_MODEL), _mod_spec(layer, 0, tps, latent), _mod_spec(layer, 1, tps, latent),
                _layer_spec(w_in, layer // 2)]
    args = [x, mod, mod, w_in]
    out_specs = [row_spec(C_WIDTH)] * 2 + [pl.BlockSpec((C_WIDTH, tm), lambda i: (0, i))]
    out_shape = [jax.ShapeDtypeStruct((rows, C_WIDTH), BF16)] * 2 + [jax.ShapeDtypeStruct((C_WIDTH, rows), BF16)]
    if latent:
        pos_spec = pl.BlockSpec((tm, LANES), lambda i: (i % tps, 0))
        in_specs += [pos_spec, pos_spec]
        args += list(rope)
    else:
        out_specs += [row_spec(C_WIDTH)] * 2
        out_shape += [jax.ShapeDtypeStruct((rows, C_WIDTH), F32)] * 2
    return pl.pallas_call(
        functools.partial(_in_odd_kernel, latent=latent),
        grid=(rows // tm,), in_specs=in_specs, out_specs=out_specs, out_shape=out_shape,
        compiler_params=_params("arbitrary"), name="in_odd",
    )(*args)


CONV_HALO = 16


def _conv_b_kernel(u_ref, w_ref, b_ref, g_ref, beta_ref, o_ref, pad_ref, *, seq_len):
    chunk = 64
    zeros = jnp.zeros((CONV_HALO, B_WIDTH), F32)
    pad_ref[0:CONV_HALO, :] = zeros
    pad_ref[CONV_HALO + seq_len:2 * CONV_HALO + seq_len, :] = zeros
    pad_ref[CONV_HALO:CONV_HALO + seq_len, :] = u_ref[...]
    centre = B_CONV_W // 2
    for r in range(seq_len // chunk):
        base = CONV_HALO + r * chunk
        cols = []
        for c in range(B_WIDTH // LANES):
            lanes = slice(c * LANES, (c + 1) * LANES)
            acc = None
            for b in range(SUBLANES):
                part = None
                for a in range(-2, 2):
                    k = SUBLANES * a + b + centre
                    if not 0 <= k < B_CONV_W:
                        continue
                    start = base + SUBLANES * a
                    term = pad_ref[start:start + chunk + SUBLANES, lanes] * w_ref[k:k + 1, lanes]
                    part = term if part is None else part + term
                part = part[b:b + chunk]
                acc = part if acc is None else acc + part
            cols.append(acc + b_ref[:, lanes])
        y = _layer_norm(jnp.concatenate(cols, axis=-1), g_ref[...], beta_ref[...])
        o_ref[r * chunk:(r + 1) * chunk, :] = (y * _sigmoid(y)).astype(BF16)


def _conv_b(u, w, b, g, beta, layer, *, seq_len):
    rows = u.shape[0]
    e = layer // 2
    return pl.pallas_call(
        functools.partial(_conv_b_kernel, seq_len=seq_len),
        grid=(rows // seq_len,),
        in_specs=[pl.BlockSpec((seq_len, B_WIDTH), lambda i: (i, 0)), _layer_spec(w, e),
                  _layer_spec(b, e), _layer_spec(g, e), _layer_spec(beta, e)],
        out_specs=pl.BlockSpec((seq_len, B_WIDTH), lambda i: (i, 0)),
        out_shape=jax.ShapeDtypeStruct((rows, B_WIDTH), BF16),
        scratch_shapes=[pltpu.VMEM((seq_len + 2 * CONV_HALO, B_WIDTH), F32)],
        compiler_params=_params("arbitrary"), name="conv_b",
    )(u, w, b, g, beta)


def _score_lookahead(latent):
    return 4 if latent else 8


def _pipelined(n, scores, consume, lookahead):
    pending = [scores(i) for i in range(min(lookahead, n))]
    outs = []
    for i in range(n):
        cur = pending.pop(0)
        if i + lookahead < n:
            pending.append(scores(i + lookahead))
        outs.append(consume(i, cur))
    return outs


def _half_lanes(x, half):
    lane = lax.broadcasted_iota(jnp.int32, x.shape, 1)
    return jnp.where((lane < HEAD_DIM) == (half == 0), x, jnp.zeros_like(x))


def _softmax_pv_t(s, vt_ones):
    v_dim = vt_ones.shape[0] - BF16_ROWS
    p = jnp.exp2(s - s.max(axis=0, keepdims=True)).astype(BF16)
    o = _dot(vt_ones, p)
    return o[0:v_dim] * (1.0 / o[v_dim:v_dim + 1])


A_VT_ROWS = HEAD_DIM + BF16_ROWS
C_VT_ROWS = C_VDIM + BF16_ROWS


def _attn_even_kernel(*refs, latent, seq_len):
    if latent:
        q_ref, k_ref, vt_ref, ck_ref, cv_ref, o_ref, kall_ref, vtall_ref = refs
    else:
        q_ref, k_ref, vt_ref, o_ref, vtall_ref = refs
        kall_ref = k_ref
    n_keys = vtall_ref.shape[1]

    @pl.when(pl.program_id(1) == 0)
    def _():
        if latent:
            kall_ref[0:seq_len, :] = k_ref[...]
            ck = ck_ref[...]
            kall_ref[seq_len:, 0:LANES] = ck.astype(BF16)
            kall_ref[seq_len:, LANES:2 * LANES] = pltpu.roll(ck, HEAD_DIM, 1).astype(BF16)
            cvt = cv_ref[...].T
        for j in range(A_KV_HEADS):
            r0 = j * A_VT_ROWS
            vtall_ref[r0:r0 + HEAD_DIM, 0:seq_len] = vt_ref[j * HEAD_DIM:(j + 1) * HEAD_DIM, :]
            if latent:
                vtall_ref[r0:r0 + HEAD_DIM, seq_len:] = cvt[j * HEAD_DIM:(j + 1) * HEAD_DIM].astype(BF16)
            vtall_ref[r0 + HEAD_DIM:r0 + A_VT_ROWS, :] = jnp.ones((BF16_ROWS, n_keys), BF16)

    group = A_HEADS // A_KV_HEADS

    def scores(h):
        j, half = h // group, h % 2
        lo = (h // 2) * LANES
        q = _half_lanes(q_ref[:, lo:lo + LANES], half)
        order = 0 if half == j else 1
        return _dot_nt(kall_ref[:, order * LANES:(order + 1) * LANES], q)

    def consume(h, s):
        r0 = (h // group) * A_VT_ROWS
        return _softmax_pv_t(s, vtall_ref[r0:r0 + A_VT_ROWS, :])

    outs = _pipelined(A_HEADS, scores, consume, _score_lookahead(latent))
    o_ref[...] = jnp.concatenate(outs, axis=0).T.astype(BF16)


def _attn_even(q, k2, vt, cache, *, seq_len, latent):
    rows = q.shape[0]
    tq = QUERY_TILE
    tps = seq_len // tq
    q_spec = pl.BlockSpec((tq, A_WIDTH), lambda b, i: (b * tps + i, 0))
    in_specs = [q_spec, pl.BlockSpec((seq_len, 2 * A_KV_WIDTH), lambda b, i: (b, 0)),
                pl.BlockSpec((A_KV_WIDTH, seq_len), lambda b, i: (0, b))]
    args = [q, k2, vt]
    n_keys = seq_len
    scratch = []
    if latent:
        past = cache[0].shape[1]
        n_keys += past
        c_spec = pl.BlockSpec((None, past, A_KV_WIDTH), lambda b, i: (b, 0, 0))
        in_specs += [c_spec, c_spec]
        args += [cache[0].reshape(-1, past, A_KV_WIDTH), cache[1].reshape(-1, past, A_KV_WIDTH)]
        scratch.append(pltpu.VMEM((n_keys, 2 * A_KV_WIDTH), BF16))
    scratch.append(pltpu.VMEM((A_KV_HEADS * A_VT_ROWS, n_keys), BF16))
    return pl.pallas_call(
        functools.partial(_attn_even_kernel, latent=latent, seq_len=seq_len),
        grid=(rows // seq_len, tps), in_specs=in_specs, out_specs=q_spec,
        out_shape=jax.ShapeDtypeStruct((rows, A_WIDTH), BF16), scratch_shapes=scratch,
        compiler_params=_params("arbitrary", "arbitrary"), name="attn_even",
    )(*args)


def _attn_odd_kernel(*refs, latent, lam_init, seq_len):
    if latent:
        q_ref, k_ref, vt_ref, ck_ref, cv_ref, dl_ref, sg_ref, o_ref, kall_ref, vtall_ref = refs
    else:
        q_ref, k_ref, vt_ref, dl_ref, sg_ref, o_ref, vtall_ref = refs
        kall_ref = k_ref
    n_keys = vtall_ref.shape[1]
    past = n_keys - seq_len

    @pl.when(pl.program_id(1) == 0)
    def _():
        if latent:
            kall_ref[0:seq_len, :] = k_ref[...]
        for h in range(C_HEADS):
            lanes = slice(h * C_VDIM, (h + 1) * C_VDIM)
            r0 = h * C_VT_ROWS
            vtall_ref[r0:r0 + C_VDIM, 0:seq_len] = vt_ref[lanes, :]
            if latent:
                kall_ref[seq_len:, lanes] = ck_ref[pl.ds(h, past, stride=C_HEADS), :].astype(BF16)
                vtall_ref[r0:r0 + C_VDIM, seq_len:] = cv_ref[pl.ds(h, past, stride=C_HEADS), :].T.astype(BF16)
            vtall_ref[r0 + C_VDIM:r0 + C_VT_ROWS, :] = jnp.ones((BF16_ROWS, n_keys), BF16)

    dl = dl_ref[...]
    lam = (jnp.exp(jnp.sum(dl[0:1] * dl[1:2], axis=-1, keepdims=True))
           - jnp.exp(jnp.sum(dl[2:3] * dl[3:4], axis=-1, keepdims=True)) + lam_init)

    def scores(i):
        lanes = slice((i // 2) * C_VDIM, (i // 2 + 1) * C_VDIM)
        return _dot_nt(kall_ref[:, lanes], _half_lanes(q_ref[:, lanes], i % 2))

    def consume(i, s):
        r0 = (i // 2) * C_VT_ROWS
        return _softmax_pv_t(s, vtall_ref[r0:r0 + C_VT_ROWS, :])

    outs = _pipelined(2 * C_HEADS, scores, consume, _score_lookahead(latent))
    for h in range(C_HEADS):
        o = outs[2 * h] - lam * outs[2 * h + 1]
        ms = jnp.mean(o * o, axis=0, keepdims=True)
        o = o * lax.rsqrt(ms + EPS) * sg_ref[...] * (1.0 - lam_init)
        o_ref[:, h * C_VDIM:(h + 1) * C_VDIM] = o.T.astype(BF16)


def _attn_odd(q, k, vt, cache, dl, sg, layer, *, seq_len, latent):
    rows = q.shape[0]
    tq = QUERY_TILE
    tps = seq_len // tq
    lam_init = 0.8 - 0.6 * math.exp(-0.3 * layer)
    q_spec = pl.BlockSpec((tq, C_WIDTH), lambda b, i: (b * tps + i, 0))
    in_specs = [q_spec, pl.BlockSpec((seq_len, C_WIDTH), lambda b, i: (b, 0)),
                pl.BlockSpec((C_WIDTH, seq_len), lambda b, i: (0, b))]
    args = [q, k, vt]
    n_keys = seq_len
    scratch = []
    if latent:
        past = cache[0].shape[1]
        n_keys += past
        c_spec = pl.BlockSpec((None, past * C_HEADS, C_VDIM), lambda b, i: (b, 0, 0))
        in_specs += [c_spec, c_spec]
        args += [cache[0].reshape(-1, past * C_HEADS, C_VDIM), cache[1].reshape(-1, past * C_HEADS, C_VDIM)]
        scratch.append(pltpu.VMEM((n_keys, C_WIDTH), BF16))
    scratch.append(pltpu.VMEM((C_HEADS * C_VT_ROWS, n_keys), BF16))
    in_specs += [_layer_spec(dl, layer // 2), _layer_spec(sg, layer // 2)]
    args += [dl, sg]
    return pl.pallas_call(
        functools.partial(_attn_odd_kernel, latent=latent, lam_init=lam_init, seq_len=seq_len),
        grid=(rows // seq_len, tps), in_specs=in_specs, out_specs=q_spec,
        out_shape=jax.ShapeDtypeStruct((rows, C_WIDTH), BF16), scratch_shapes=scratch,
        compiler_params=_params("arbitrary", "arbitrary"), name="attn_odd",
    )(*args)


def _neighbour_rows(u):
    n = u.shape[0]
    sub = lax.broadcasted_iota(jnp.int32, (SUBLANES, u.shape[1]), 0)
    prev = pltpu.roll(u, 1, 0)
    nxt = pltpu.roll(u, n - 1, 0)
    prev = jnp.concatenate([jnp.where(sub == 0, 0.0, prev[0:SUBLANES]), prev[SUBLANES:]], axis=0)
    nxt = jnp.concatenate([nxt[:n - SUBLANES], jnp.where(sub == SUBLANES - 1, 0.0, nxt[n - SUBLANES:])], axis=0)
    return prev, nxt


def _mix_ffn_kernel(*refs, n_parts, seq_len):
    parts = refs[:n_parts]
    (wo_ref, x_ref, g1_ref, sh_ref, sc_ref, g2_ref, lng_ref, lnb_ref, wu_ref, cw_ref, cb_ref, wd_ref,
     o_ref, x1_ref, a_ref) = refs[n_parts:]
    width = wo_ref.shape[0] // n_parts
    spans = [slice(r, r + seq_len) for r in range(0, x_ref.shape[0], seq_len)]

    def conv(u, lo):
        cols = slice(lo, lo + FF_CHUNK)
        prev, nxt = _neighbour_rows(u)
        return prev * cw_ref[0:1, cols] + u * cw_ref[1:2, cols] + nxt * cw_ref[2:3, cols] + cb_ref[:, cols]

    hs = []
    for rows in spans:
        mix = _dot(parts[0][rows, :], wo_ref[0:width, :])
        for j in range(1, n_parts):
            mix = mix + _dot(parts[j][rows, :], wo_ref[j * width:(j + 1) * width, :])
        x1_ref[rows, :] = _layer_norm(ALPHA * x_ref[rows, :] + g1_ref[...] * mix, lng_ref[0:1, :], lnb_ref[0:1, :])
        hs.append((x1_ref[rows, :] * (1.0 + sc_ref[...]) + sh_ref[...]).astype(BF16))
    for rows, h in zip(spans, hs):
        for j in range(FF_CHUNKS):
            lo = j * FF_CHUNK
            gate = conv(_dot(h, wu_ref[:, lo:lo + FF_CHUNK]), lo)
            val = conv(_dot(h, wu_ref[:, D_FF + lo:D_FF + lo + FF_CHUNK]), D_FF + lo)
            a_ref[rows, lo:lo + FF_CHUNK] = (gate * _sigmoid(gate) * val).astype(BF16)
        y = ALPHA * x1_ref[rows, :] + g2_ref[...] * _dot(a_ref[rows, :], wd_ref[...])
        o_ref[rows, :] = _layer_norm(y, lng_ref[1:2, :], lnb_ref[1:2, :])


def _mix_ffn(parts, w_out, x, mod, layer, ln_g, ln_b, w_up, cw, cb, w_down, *, seq_len, latent):
    rows = x.shape[0]
    tm = seq_len if latent else FFN_CONTEXT_SEQS * seq_len
    row_spec = lambda w: pl.BlockSpec((tm, w), lambda i: (i, 0))
    res = lambda a, l: _layer_spec(a, l, single_buffer=True)
    in_specs = [row_spec(p.shape[1]) for p in parts]
    in_specs += [res(w_out, layer // 2), row_spec(D_MODEL), _mod_spec(layer, 2, 1, latent),
                 _mod_spec(layer, 3, 1, latent), _mod_spec(layer, 4, 1, latent), _mod_spec(layer, 5, 1, latent),
                 res(ln_g, layer), res(ln_b, layer), res(w_up, layer), res(cw, layer), res(cb, layer),
                 res(w_down, layer)]
    return pl.pallas_call(
        functools.partial(_mix_ffn_kernel, n_parts=len(parts), seq_len=seq_len),
        grid=(rows // tm,), in_specs=in_specs, out_specs=row_spec(D_MODEL),
        out_shape=jax.ShapeDtypeStruct((rows, D_MODEL), F32),
        scratch_shapes=[pltpu.VMEM((tm, D_MODEL), F32), pltpu.VMEM((tm, D_FF), BF16)],
        compiler_params=_params("arbitrary"), name="mix_ffn",
    )(*parts, w_out, x, mod, mod, mod, mod, ln_g, ln_b, w_up, cw, cb, w_down)


def _rope_lane_tables(n):
    t = jnp.arange(n)
    half = HEAD_DIM // 2
    inv = 1.0 / (ROPE_THETA ** (jnp.arange(0, half, 2, dtype=F32) / half))
    ang_r = (t // GRID_W).astype(F32)[:, None] * inv
    ang_c = (t % GRID_W).astype(F32)[:, None] * inv
    cos = jnp.concatenate([jnp.cos(ang_r)] * 2 + [jnp.cos(ang_c)] * 2, axis=-1)
    sin = jnp.concatenate([-jnp.sin(ang_r), jnp.sin(ang_r), -jnp.sin(ang_c), jnp.sin(ang_c)], axis=-1)
    reps = LANES // HEAD_DIM
    return jnp.tile(cos, (1, reps)), jnp.tile(sin, (1, reps))


def kernel(x_prompt, x_sample, cache_a_k_l0, cache_a_v_l0, cache_c_k_l1, cache_c_v_l1, cache_a_k_l2,
           cache_a_v_l2, cache_c_k_l3, cache_c_v_l3, c, c_ctx, w_ada, b_ada, ln_g, ln_b, w_in_even,
           q_norm_g, k_norm_g, b_conv_w, b_conv_b, b_norm_g, b_norm_b, w_out_even, w_in_odd,
           diff_lambda, subln_g, w_out_odd, w_up, ffn_conv_w, ffn_conv_b, w_down):
    batch, seq, d = x_prompt.shape
    dec_batch, dec_seq, _ = x_sample.shape
    assert d == D_MODEL and 1 + dec_batch <= MOD_ROWS
    caches = [(cache_a_k_l0, cache_a_v_l0), (cache_c_k_l1, cache_c_v_l1),
              (cache_a_k_l2, cache_a_v_l2), (cache_c_k_l3, cache_c_v_l3)]

    cond = jnp.zeros((MOD_ROWS, d), F32).at[0].set(c_ctx).at[1:1 + dec_batch].set(c)
    mod = _modulation(cond, w_ada, b_ada).reshape(DEPTH * MOD_ROWS * 6, 1, d)
    rope = _rope_lane_tables(dec_seq)

    w_in_even_b = w_in_even.astype(BF16)
    w_out_even_b = w_out_even.astype(BF16)
    w_in_odd_b = w_in_odd.astype(BF16)
    w_out_odd_b = w_out_odd.astype(BF16)
    w_up_b = w_up.astype(BF16)
    w_down_b = w_down.astype(BF16)
    cb = ffn_conv_b[:, None, :]
    qg = jnp.tile(q_norm_g, (1, LANES // HEAD_DIM))
    kg = jnp.tile(k_norm_g, (1, LANES // HEAD_DIM))
    conv_b = b_conv_b[:, None, :]
    norm_g = b_norm_g[:, None, :]
    norm_b = b_norm_b[:, None, :]
    sub_g = subln_g[:, :, None]

    streams = [
        [x_prompt.reshape(batch * seq, d), seq, False],
        [x_sample.reshape(dec_batch * dec_seq, d), dec_seq, True],
    ]
    ctx_state = []
    for layer in range(DEPTH):
        for stream in streams:
            x, s, latent = stream
            kw = dict(seq_len=s, latent=latent)
            if layer % 2 == 0:
                e = layer // 2
                res = _in_even(x, mod, layer, w_in_even_b, qg[e][None], kg[e][None], rope, **kw)
                q, k2, vt, u = res[:4]
                if not latent:
                    ctx_state.append((res[4].reshape(batch, seq, A_KV_HEADS, HEAD_DIM),
                                      res[5].reshape(batch, seq, A_KV_HEADS, HEAD_DIM)))
                attn = _attn_even(q, k2, vt, caches[layer], **kw)
                conv = _conv_b(u, b_conv_w, conv_b, norm_g, norm_b, layer, seq_len=s)
                parts, w_out = [attn, conv], w_out_even_b
            else:
                res = _in_odd(x, mod, layer, w_in_odd_b, rope, **kw)
                q, k, vt = res[:3]
                if not latent:
                    ctx_state.append((res[3].reshape(batch, seq, C_HEADS, C_VDIM),
                                      res[4].reshape(batch, seq, C_HEADS, C_VDIM)))
                attn = _attn_odd(q, k, vt, caches[layer], diff_lambda, sub_g, layer, **kw)
                parts, w_out = [attn], w_out_odd_b
            stream[0] = _mix_ffn(parts, w_out, x, mod, layer, ln_g, ln_b, w_up_b, ffn_conv_w, cb, w_down_b, **kw)
    y_prompt = streams[0][0].reshape(batch, seq, d)
    y_sample = streams[1][0].reshape(dec_batch, dec_seq, d)
    (a_k0, a_v0), (c_k1, c_v1), (a_k2, a_v2), (c_k3, c_v3) = ctx_state
    return (y_prompt, y_sample, a_k0, a_v0, c_k1, c_v1, a_k2, a_v2, c_k3, c_v3)
```

```python
import functools
import math

import jax
import jax.numpy as jnp
from jax import lax
from jax.experimental import pallas as pl
from jax.experimental.pallas import tpu as pltpu

F32 = jnp.float32
BF16 = jnp.bfloat16

D_MODEL = 1024
DEPTH = 4
GRID_W = 64
HEAD_DIM = 64
A_HEADS = 8
A_KV_HEADS = 2
A_WIDTH = A_HEADS * HEAD_DIM
A_KV_WIDTH = A_KV_HEADS * HEAD_DIM
B_WIDTH = D_MODEL - A_WIDTH
B_CONV_W = 31
C_HEADS = D_MODEL // (2 * HEAD_DIM)
C_VDIM = 2 * HEAD_DIM
C_WIDTH = C_HEADS * C_VDIM
D_FF = 2816
ROPE_THETA = 10000.0
EPS = 1e-6
ATTN_SCALE = HEAD_DIM ** -0.5
LOG2E = math.log2(math.e)
Q_SCALE = ATTN_SCALE * LOG2E
ALPHA = (2 * DEPTH) ** 0.25
EVEN_IN = A_WIDTH + 2 * A_KV_WIDTH + 2 * B_WIDTH
ODD_IN = 3 * C_WIDTH

LANES = 128
SUBLANES = 8
BF16_ROWS = 16
MXU_DIM = 256
FF_CHUNK = MXU_DIM
FF_CHUNKS = D_FF // FF_CHUNK
MOD_ROWS = 16
QUERY_TILE = 256
PROJ_SUB = 256
PROJ_TILE = 1024
FFN_CONTEXT_SEQS = 2
NORM_BLOCK = 256
VMEM_LIMIT = 56 * 1024 * 1024


def _params(*sem):
    return pltpu.CompilerParams(dimension_semantics=sem, vmem_limit_bytes=VMEM_LIMIT)


def _dot(a, b):
    return jnp.dot(a, b, preferred_element_type=F32)


def _dot_nt(a, b):
    return lax.dot_general(a, b, (((1,), (1,)), ((), ())), preferred_element_type=F32)


def _sigmoid(x):
    return 1.0 / (1.0 + jnp.exp2(x * (-LOG2E)))


def _layer_norm(y, g, b):
    mu = jnp.mean(y, axis=-1, keepdims=True)
    d = y - mu
    var = jnp.mean(d * d, axis=-1, keepdims=True)
    return d * lax.rsqrt(var + EPS) * g + b


def _group_mean_sq(x, group):
    r = lax.broadcasted_iota(jnp.int32, (LANES, LANES), 0) // group
    c = lax.broadcasted_iota(jnp.int32, (LANES, LANES), 1) // group
    ones = jnp.where(r == c, 1.0, 0.0).astype(BF16)
    sq = x * x
    hi = sq.astype(BF16)
    lo = (sq - hi.astype(F32)).astype(BF16)
    return (_dot(hi, ones) + _dot(lo, ones)) * (1.0 / group)


def _rope(x, cos, sin):
    lane = lax.broadcasted_iota(jnp.int32, x.shape, 1)
    partner = jnp.where((lane & 16) == 0, pltpu.roll(x, LANES - 16, 1), pltpu.roll(x, 16, 1))
    return x * cos + partner * sin


def _layer_spec(a, layer, single_buffer=False):
    mode = dict(pipeline_mode=pl.Buffered(1)) if single_buffer else {}
    return pl.BlockSpec((None,) + a.shape[1:], lambda *_: (layer,) + (0,) * (a.ndim - 1), **mode)


def _mod_kernel(c_ref, w_ref, b_ref, o_ref):
    c = c_ref[...]
    s = (c * _sigmoid(c)).astype(BF16)
    o_ref[...] = _dot(s, w_ref[...].astype(BF16)) + b_ref[...]


def _modulation(cond, w_ada, b_ada):
    tn = 1536
    n = 6 * D_MODEL
    return pl.pallas_call(
        _mod_kernel,
        grid=(DEPTH, n // tn),
        in_specs=[
            pl.BlockSpec((MOD_ROWS, D_MODEL), lambda l, j: (0, 0)),
            pl.BlockSpec((None, D_MODEL, tn), lambda l, j: (l, 0, j)),
            pl.BlockSpec((None, 1, tn), lambda l, j: (l, 0, j)),
        ],
        out_specs=pl.BlockSpec((None, MOD_ROWS, tn), lambda l, j: (l, 0, j)),
        out_shape=jax.ShapeDtypeStruct((DEPTH, MOD_ROWS, n), F32),
        compiler_params=_params("arbitrary", "arbitrary"), name="adaln_mod",
    )(cond, w_ada, b_ada.reshape(DEPTH, 1, n))


def _mod_spec(layer, which, tiles_per_seq, latent):
    def index(i, *_):
        row = 1 + i // tiles_per_seq if latent else 0
        return ((layer * MOD_ROWS + row) * 6 + which, 0, 0)
    return pl.BlockSpec((None, 1, D_MODEL), index)


def _sub_tile_projections(x_ref, sh_ref, sc_ref, w_ref):
    h = (x_ref[...] * (1.0 + sc_ref[...]) + sh_ref[...]).astype(BF16)
    subs = [slice(r, r + PROJ_SUB) for r in range(0, h.shape[0], PROJ_SUB)]
    return list(zip(subs, [_dot(h[rows], w_ref[...]) for rows in subs]))

def _in_even_kernel(*refs, latent):
    if latent:
        x_ref, sh_ref, sc_ref, w_ref, qg_ref, kg_ref, cos_ref, sin_ref, q_ref, k_ref, v_ref, u_ref = refs
    else:
        x_ref, sh_ref, sc_ref, w_ref, qg_ref, kg_ref, q_ref, k_ref, v_ref, u_ref, ko_ref, vo_ref = refs
    for rows, proj in _sub_tile_projections(x_ref, sh_ref, sc_ref, w_ref):
        for c in range((A_WIDTH + A_KV_WIDTH) // LANES):
            lo = c * LANES
            x = proj[:, lo:lo + LANES]
            is_q = lo < A_WIDTH
            gain = qg_ref[...] if is_q else kg_ref[...]
            x = x * lax.rsqrt(_group_mean_sq(x, HEAD_DIM) + EPS) * gain
            if not latent and not is_q:
                ko_ref[rows, :] = x
            if latent:
                x = _rope(x, cos_ref[rows, :], sin_ref[rows, :])
            if is_q:
                q_ref[rows, lo:lo + LANES] = (x * Q_SCALE).astype(BF16)
            else:
                k_ref[rows, 0:LANES] = x.astype(BF16)
                k_ref[rows, LANES:2 * LANES] = pltpu.roll(x, HEAD_DIM, 1).astype(BF16)
        v = proj[:, A_WIDTH + A_KV_WIDTH:A_WIDTH + 2 * A_KV_WIDTH]
        v_ref[:, rows] = v.T.astype(BF16)
        if not latent:
            vo_ref[rows, :] = v
        u0 = A_WIDTH + 2 * A_KV_WIDTH
        ua = proj[:, u0:u0 + B_WIDTH]
        ub = proj[:, u0 + B_WIDTH:u0 + 2 * B_WIDTH]
        u_ref[rows, :] = ua * _sigmoid(ub)


def _in_even(x, mod, layer, w_in, qg, kg, rope, *, seq_len, latent):
    rows = x.shape[0]
    tm = PROJ_TILE
    tps = max(seq_len // tm, 1)
    row_spec = lambda w: pl.BlockSpec((tm, w), lambda i: (i, 0))
    full = lambda a: pl.BlockSpec(a.shape, lambda i: (0,) * a.ndim)
    in_specs = [row_spec(D_MODEL), _mod_spec(layer, 0, tps, latent), _mod_spec(layer, 1, tps, latent),
                _layer_spec(w_in, layer // 2), full(qg), full(kg)]
    args = [x, mod, mod, w_in, qg, kg]
    out_specs = [row_spec(A_WIDTH), row_spec(2 * A_KV_WIDTH),
                 pl.BlockSpec((A_KV_WIDTH, tm), lambda i: (0, i)), row_spec(B_WIDTH)]
    out_shape = [jax.ShapeDtypeStruct((rows, A_WIDTH), BF16), jax.ShapeDtypeStruct((rows, 2 * A_KV_WIDTH), BF16),
                 jax.ShapeDtypeStruct((A_KV_WIDTH, rows), BF16), jax.ShapeDtypeStruct((rows, B_WIDTH), F32)]
    if latent:
        pos_spec = pl.BlockSpec((tm, LANES), lambda i: (i % tps, 0))
        in_specs += [pos_spec, pos_spec]
        args += list(rope)
    else:
        out_specs += [row_spec(A_KV_WIDTH), row_spec(A_KV_WIDTH)]
        out_shape += [jax.ShapeDtypeStruct((rows, A_KV_WIDTH), F32)] * 2
    return pl.pallas_call(
        functools.partial(_in_even_kernel, latent=latent),
        grid=(rows // tm,), in_specs=in_specs, out_specs=out_specs, out_shape=out_shape,
        compiler_params=_params("arbitrary"), name="in_even",
    )(*args)


def _in_odd_kernel(*refs, latent):
    if latent:
        x_ref, sh_ref, sc_ref, w_ref, cos_ref, sin_ref, q_ref, k_ref, v_ref = refs
    else:
        x_ref, sh_ref, sc_ref, w_ref, q_ref, k_ref, v_ref, ko_ref, vo_ref = refs
    for rows, proj in _sub_tile_projections(x_ref, sh_ref, sc_ref, w_ref):
        for c in range(2 * C_WIDTH // LANES):
            lo = c * LANES
            x = proj[:, lo:lo + LANES]
            if latent:
                x = _rope(x, cos_ref[rows, :], sin_ref[rows, :])
            if lo < C_WIDTH:
                q_ref[rows, lo:lo + LANES] = (x * Q_SCALE).astype(BF16)
            else:
                k_ref[rows, lo - C_WIDTH:lo - C_WIDTH + LANES] = x.astype(BF16)
        v = proj[:, 2 * C_WIDTH:]
        v_ref[:, rows] = v.T.astype(BF16)
        if not latent:
            ko_ref[rows, :] = proj[:, C_WIDTH:2 * C_WIDTH]
            vo_ref[rows, :] = v


def _in_odd(x, mod, layer, w_in, rope, *, seq_len, latent):
    rows = x.shape[0]
    tm = PROJ_TILE
    tps = max(seq_len // tm, 1)
    row_spec = lambda w: pl.BlockSpec((tm, w), lambda i: (i, 0))
    in_specs = [row_spec(D_MODEL), _mod_spec(layer, 0, tps, latent), _mod_spec(layer, 1, tps, latent),
                _layer_spec(w_in, layer // 2)]
    args = [x, mod, mod, w_in]
    out_specs = [row_spec(C_WIDTH)] * 2 + [pl.BlockSpec((C_WIDTH, tm), lambda i: (0, i))]
    out_shape = [jax.ShapeDtypeStruct((rows, C_WIDTH), BF16)] * 2 + [jax.ShapeDtypeStruct((C_WIDTH, rows), BF16)]
    if latent:
        pos_spec = pl.BlockSpec((tm, LANES), lambda i: (i % tps, 0))
        in_specs += [pos_spec, pos_spec]
        args += list(rope)
    else:
        out_specs += [row_spec(C_WIDTH)] * 2
        out_shape += [jax.ShapeDtypeStruct((rows, C_WIDTH), F32)] * 2
    return pl.pallas_call(
        functools.partial(_in_odd_kernel, latent=latent),
        grid=(rows // tm,), in_specs=in_specs, out_specs=out_specs, out_shape=out_shape,
        compiler_params=_params("arbitrary"), name="in_odd",
    )(*args)


CONV_HALO = 16


def _conv_b_kernel(u_ref, w_ref, b_ref, g_ref, beta_ref, o_ref, pad_ref, *, seq_len):
    chunk = 128
    zeros = jnp.zeros((CONV_HALO, B_WIDTH), F32)
    pad_ref[0:CONV_HALO, :] = zeros
    pad_ref[CONV_HALO + seq_len:2 * CONV_HALO + seq_len, :] = zeros
    pad_ref[CONV_HALO:CONV_HALO + seq_len, :] = u_ref[...]
    centre = B_CONV_W // 2
    for r in range(seq_len // chunk):
        base = CONV_HALO + r * chunk
        cols = []
        for c in range(B_WIDTH // LANES):
            lanes = slice(c * LANES, (c + 1) * LANES)
            acc = None
            for b in range(SUBLANES):
                part = None
                for a in range(-2, 2):
                    k = SUBLANES * a + b + centre
                    if not 0 <= k < B_CONV_W:
                        continue
                    start = base + SUBLANES * a
                    term = pad_ref[start:start + chunk + SUBLANES, lanes] * w_ref[k:k + 1, lanes]
                    part = term if part is None else part + term
                part = part[b:b + chunk]
                acc = part if acc is None else acc + part
            cols.append(acc + b_ref[:, lanes])
        y = _layer_norm(jnp.concatenate(cols, axis=-1), g_ref[...], beta_ref[...])
        o_ref[r * chunk:(r + 1) * chunk, :] = (y * _sigmoid(y)).astype(BF16)


def _conv_b(u, w, b, g, beta, layer, *, seq_len):
    rows = u.shape[0]
    e = layer // 2
    return pl.pallas_call(
        functools.partial(_conv_b_kernel, seq_len=seq_len),
        grid=(rows // seq_len,),
        in_specs=[pl.BlockSpec((seq_len, B_WIDTH), lambda i: (i, 0)), _layer_spec(w, e),
                  _layer_spec(b, e), _layer_spec(g, e), _layer_spec(beta, e)],
        out_specs=pl.BlockSpec((seq_len, B_WIDTH), lambda i: (i, 0)),
        out_shape=jax.ShapeDtypeStruct((rows, B_WIDTH), BF16),
        scratch_shapes=[pltpu.VMEM((seq_len + 2 * CONV_HALO, B_WIDTH), F32)],
        compiler_params=_params("arbitrary"), name="conv_b",
    )(u, w, b, g, beta)


def _score_lookahead(latent):
    return 4 if latent else 8


def _pipelined(n, scores, consume, lookahead):
    pending = [scores(i) for i in range(min(lookahead, n))]
    outs = []
    for i in range(n):
        cur = pending.pop(0)
        if i + lookahead < n:
            pending.append(scores(i + lookahead))
        outs.append(consume(i, cur))
    return outs


def _half_lanes(x, half):
    lane = lax.broadcasted_iota(jnp.int32, x.shape, 1)
    return jnp.where((lane < HEAD_DIM) == (half == 0), x, jnp.zeros_like(x))


def _softmax_pv_t(s, vt_ones):
    v_dim = vt_ones.shape[0] - BF16_ROWS
    p = jnp.exp2(s - s.max(axis=0, keepdims=True)).astype(BF16)
    o = _dot(vt_ones, p)
    return o[0:v_dim] * (1.0 / o[v_dim:v_dim + 1])


A_VT_ROWS = HEAD_DIM + BF16_ROWS
C_VT_ROWS = C_VDIM + BF16_ROWS


def _attn_even_kernel(*refs, latent, seq_len):
    if latent:
        q_ref, k_ref, vt_ref, ck_ref, cv_ref, o_ref, kall_ref, vtall_ref = refs
    else:
        q_ref, k_ref, vt_ref, o_ref, vtall_ref = refs
        kall_ref = k_ref
    n_keys = vtall_ref.shape[1]

    @pl.when(pl.program_id(1) == 0)
    def _():
        if latent:
            kall_ref[0:seq_len, :] = k_ref[...]
            ck = ck_ref[...]
            kall_ref[seq_len:, 0:LANES] = ck.astype(BF16)
            kall_ref[seq_len:, LANES:2 * LANES] = pltpu.roll(ck, HEAD_DIM, 1).astype(BF16)
            cvt = cv_ref[...].T
        for j in range(A_KV_HEADS):
            r0 = j * A_VT_ROWS
            vtall_ref[r0:r0 + HEAD_DIM, 0:seq_len] = vt_ref[j * HEAD_DIM:(j + 1) * HEAD_DIM, :]
            if latent:
                vtall_ref[r0:r0 + HEAD_DIM, seq_len:] = cvt[j * HEAD_DIM:(j + 1) * HEAD_DIM].astype(BF16)
            vtall_ref[r0 + HEAD_DIM:r0 + A_VT_ROWS, :] = jnp.ones((BF16_ROWS, n_keys), BF16)

    group = A_HEADS // A_KV_HEADS

    def scores(h):
        j, half = h // group, h % 2
        lo = (h // 2) * LANES
        q = _half_lanes(q_ref[:, lo:lo + LANES], half)
        order = 0 if half == j else 1
        return _dot_nt(kall_ref[:, order * LANES:(order + 1) * LANES], q)

    def consume(h, s):
        r0 = (h // group) * A_VT_ROWS
        return _softmax_pv_t(s, vtall_ref[r0:r0 + A_VT_ROWS, :])

    outs = _pipelined(A_HEADS, scores, consume, _score_lookahead(latent))
    o_ref[...] = jnp.concatenate(outs, axis=0).T.astype(BF16)


def _attn_even(q, k2, vt, cache, *, seq_len, latent):
    rows = q.shape[0]
    tq = QUERY_TILE
    tps = seq_len // tq
    q_spec = pl.BlockSpec((tq, A_WIDTH), lambda b, i: (b * tps + i, 0))
    in_specs = [q_spec, pl.BlockSpec((seq_len, 2 * A_KV_WIDTH), lambda b, i: (b, 0)),
                pl.BlockSpec((A_KV_WIDTH, seq_len), lambda b, i: (0, b))]
    args = [q, k2, vt]
    n_keys = seq_len
    scratch = []
    if latent:
        past = cache[0].shape[1]
        n_keys += past
        c_spec = pl.BlockSpec((None, past, A_KV_WIDTH), lambda b, i: (b, 0, 0))
        in_specs += [c_spec, c_spec]
        args += [cache[0].reshape(-1, past, A_KV_WIDTH), cache[1].reshape(-1, past, A_KV_WIDTH)]
        scratch.append(pltpu.VMEM((n_keys, 2 * A_KV_WIDTH), BF16))
    scratch.append(pltpu.VMEM((A_KV_HEADS * A_VT_ROWS, n_keys), BF16))
    return pl.pallas_call(
        functools.partial(_attn_even_kernel, latent=latent, seq_len=seq_len),
        grid=(rows // seq_len, tps), in_specs=in_specs, out_specs=q_spec,
        out_shape=jax.ShapeDtypeStruct((rows, A_WIDTH), BF16), scratch_shapes=scratch,
        compiler_params=_params("arbitrary", "arbitrary"), name="attn_even",
    )(*args)


def _attn_odd_kernel(*refs, latent, lam_init, seq_len):
    if latent:
        q_ref, k_ref, vt_ref, ck_ref, cv_ref, dl_ref, sg_ref, o_ref, kall_ref, vtall_ref = refs
    else:
        q_ref, k_ref, vt_ref, dl_ref, sg_ref, o_ref, vtall_ref = refs
        kall_ref = k_ref
    n_keys = vtall_ref.shape[1]
    past = n_keys - seq_len

    @pl.when(pl.program_id(1) == 0)
    def _():
        if latent:
            kall_ref[0:seq_len, :] = k_ref[...]
        for h in range(C_HEADS):
            lanes = slice(h * C_VDIM, (h + 1) * C_VDIM)
            r0 = h * C_VT_ROWS
            vtall_ref[r0:r0 + C_VDIM, 0:seq_len] = vt_ref[lanes, :]
            if latent:
                kall_ref[seq_len:, lanes] = ck_ref[pl.ds(h, past, stride=C_HEADS), :].astype(BF16)
                vtall_ref[r0:r0 + C_VDIM, seq_len:] = cv_ref[pl.ds(h, past, stride=C_HEADS), :].T.astype(BF16)
            vtall_ref[r0 + C_VDIM:r0 + C_VT_ROWS, :] = jnp.ones((BF16_ROWS, n_keys), BF16)

    dl = dl_ref[...]
    lam = (jnp.exp(jnp.sum(dl[0:1] * dl[1:2], axis=-1, keepdims=True))
           - jnp.exp(jnp.sum(dl[2:3] * dl[3:4], axis=-1, keepdims=True)) + lam_init)

    def scores(i):
        lanes = slice((i // 2) * C_VDIM, (i // 2 + 1) * C_VDIM)
        return _dot_nt(kall_ref[:, lanes], _half_lanes(q_ref[:, lanes], i % 2))

    def consume(i, s):
        r0 = (i // 2) * C_VT_ROWS
        return _softmax_pv_t(s, vtall_ref[r0:r0 + C_VT_ROWS, :])

    outs = _pipelined(2 * C_HEADS, scores, consume, _score_lookahead(latent))
    for h in range(C_HEADS):
        o = outs[2 * h] - lam * outs[2 * h + 1]
        ms = jnp.mean(o * o, axis=0, keepdims=True)
        o = o * lax.rsqrt(ms + EPS) * sg_ref[...] * (1.0 - lam_init)
        o_ref[:, h * C_VDIM:(h + 1) * C_VDIM] = o.T.astype(BF16)


def _attn_odd(q, k, vt, cache, dl, sg, layer, *, seq_len, latent):
    rows = q.shape[0]
    tq = QUERY_TILE
    tps = seq_len // tq
    lam_init = 0.8 - 0.6 * math.exp(-0.3 * layer)
    q_spec = pl.BlockSpec((tq, C_WIDTH), lambda b, i: (b * tps + i, 0))
    in_specs = [q_spec, pl.BlockSpec((seq_len, C_WIDTH), lambda b, i: (b, 0)),
                pl.BlockSpec((C_WIDTH, seq_len), lambda b, i: (0, b))]
    args = [q, k, vt]
    n_keys = seq_len
    scratch = []
    if latent:
        past = cache[0].shape[1]
        n_keys += past
        c_spec = pl.BlockSpec((None, past * C_HEADS, C_VDIM), lambda b, i: (b, 0, 0))
        in_specs += [c_spec, c_spec]
        args += [cache[0].reshape(-1, past * C_HEADS, C_VDIM), cache[1].reshape(-1, past * C_HEADS, C_VDIM)]
        scratch.append(pltpu.VMEM((n_keys, C_WIDTH), BF16))
    scratch.append(pltpu.VMEM((C_HEADS * C_VT_ROWS, n_keys), BF16))
    in_specs += [_layer_spec(dl, layer // 2), _layer_spec(sg, layer // 2)]
    args += [dl, sg]
    return pl.pallas_call(
        functools.partial(_attn_odd_kernel, latent=latent, lam_init=lam_init, seq_len=seq_len),
        grid=(rows // seq_len, tps), in_specs=in_specs, out_specs=q_spec,
        out_shape=jax.ShapeDtypeStruct((rows, C_WIDTH), BF16), scratch_shapes=scratch,
        compiler_params=_params("arbitrary", "arbitrary"), name="attn_odd",
    )(*args)


def _neighbour_rows(u):
    n = u.shape[0]
    sub = lax.broadcasted_iota(jnp.int32, (SUBLANES, u.shape[1]), 0)
    prev = pltpu.roll(u, 1, 0)
    nxt = pltpu.roll(u, n - 1, 0)
    prev = jnp.concatenate([jnp.where(sub == 0, 0.0, prev[0:SUBLANES]), prev[SUBLANES:]], axis=0)
    nxt = jnp.concatenate([nxt[:n - SUBLANES], jnp.where(sub == SUBLANES - 1, 0.0, nxt[n - SUBLANES:])], axis=0)
    return prev, nxt


def _mix_ffn_kernel(*refs, n_parts, seq_len):
    parts = refs[:n_parts]
    (wo_ref, x_ref, g1_ref, sh_ref, sc_ref, g2_ref, lng_ref, lnb_ref, wu_ref, cw_ref, cb_ref, wd_ref,
     o_ref, x1_ref, a_ref) = refs[n_parts:]
    width = wo_ref.shape[0] // n_parts
    spans = [slice(r, r + seq_len) for r in range(0, x_ref.shape[0], seq_len)]

    def conv(u, lo):
        cols = slice(lo, lo + FF_CHUNK)
        prev, nxt = _neighbour_rows(u)
        return prev * cw_ref[0:1, cols] + u * cw_ref[1:2, cols] + nxt * cw_ref[2:3, cols] + cb_ref[:, cols]

    def blocks(rows):
        return [slice(r, r + NORM_BLOCK) for r in range(rows.start, rows.stop, NORM_BLOCK)]

    hs = []
    for rows in spans:
        for blk in blocks(rows):
            mix = _dot(parts[0][blk, :], wo_ref[0:width, :])
            for j in range(1, n_parts):
                mix = mix + _dot(parts[j][blk, :], wo_ref[j * width:(j + 1) * width, :])
            x1_ref[blk, :] = _layer_norm(ALPHA * x_ref[blk, :] + g1_ref[...] * mix, lng_ref[0:1, :], lnb_ref[0:1, :])
        hs.append((x1_ref[rows, :] * (1.0 + sc_ref[...]) + sh_ref[...]).astype(BF16))
    for rows, h in zip(spans, hs):
        for j in range(FF_CHUNKS):
            lo = j * FF_CHUNK
            gate = conv(_dot(h, wu_ref[:, lo:lo + FF_CHUNK]), lo)
            val = conv(_dot(h, wu_ref[:, D_FF + lo:D_FF + lo + FF_CHUNK]), D_FF + lo)
            a_ref[rows, lo:lo + FF_CHUNK] = (gate * _sigmoid(gate) * val).astype(BF16)
        for blk in blocks(rows):
            y = ALPHA * x1_ref[blk, :] + g2_ref[...] * _dot(a_ref[blk, :], wd_ref[...])
            o_ref[blk, :] = _layer_norm(y, lng_ref[1:2, :], lnb_ref[1:2, :])


def _mix_ffn(parts, w_out, x, mod, layer, ln_g, ln_b, w_up, cw, cb, w_down, *, seq_len, latent):
    rows = x.shape[0]
    tm = seq_len if latent else FFN_CONTEXT_SEQS * seq_len
    row_spec = lambda w: pl.BlockSpec((tm, w), lambda i: (i, 0))
    res = lambda a, l: _layer_spec(a, l, single_buffer=True)
    in_specs = [row_spec(p.shape[1]) for p in parts]
    in_specs += [res(w_out, layer // 2), row_spec(D_MODEL), _mod_spec(layer, 2, 1, latent),
                 _mod_spec(layer, 3, 1, latent), _mod_spec(layer, 4, 1, latent), _mod_spec(layer, 5, 1, latent),
                 res(ln_g, layer), res(ln_b, layer), res(w_up, layer), res(cw, layer), res(cb, layer),
                 res(w_down, layer)]
    return pl.pallas_call(
        functools.partial(_mix_ffn_kernel, n_parts=len(parts), seq_len=seq_len),
        grid=(rows // tm,), in_specs=in_specs, out_specs=row_spec(D_MODEL),
        out_shape=jax.ShapeDtypeStruct((rows, D_MODEL), F32),
        scratch_shapes=[pltpu.VMEM((tm, D_MODEL), F32), pltpu.VMEM((tm, D_FF), BF16)],
        compiler_params=_params("arbitrary"), name="mix_ffn",
    )(*parts, w_out, x, mod, mod, mod, mod, ln_g, ln_b, w_up, cw, cb, w_down)


def _rope_lane_tables(n):
    t = jnp.arange(n)
    half = HEAD_DIM // 2
    inv = 1.0 / (ROPE_THETA ** (jnp.arange(0, half, 2, dtype=F32) / half))
    ang_r = (t // GRID_W).astype(F32)[:, None] * inv
    ang_c = (t % GRID_W).astype(F32)[:, None] * inv
    cos = jnp.concatenate([jnp.cos(ang_r)] * 2 + [jnp.cos(ang_c)] * 2, axis=-1)
    sin = jnp.concatenate([-jnp.sin(ang_r), jnp.sin(ang_r), -jnp.sin(ang_c), jnp.sin(ang_c)], axis=-1)
    reps = LANES // HEAD_DIM
    return jnp.tile(cos, (1, reps)), jnp.tile(sin, (1, reps))


def kernel(x_prompt, x_sample, cache_a_k_l0, cache_a_v_l0, cache_c_k_l1, cache_c_v_l1, cache_a_k_l2,
           cache_a_v_l2, cache_c_k_l3, cache_c_v_l3, c, c_ctx, w_ada, b_ada, ln_g, ln_b, w_in_even,
           q_norm_g, k_norm_g, b_conv_w, b_conv_b, b_norm_g, b_norm_b, w_out_even, w_in_odd,
           diff_lambda, subln_g, w_out_odd, w_up, ffn_conv_w, ffn_conv_b, w_down):
    batch, seq, d = x_prompt.shape
    dec_batch, dec_seq, _ = x_sample.shape
    assert d == D_MODEL and 1 + dec_batch <= MOD_ROWS
    caches = [(cache_a_k_l0, cache_a_v_l0), (cache_c_k_l1, cache_c_v_l1),
              (cache_a_k_l2, cache_a_v_l2), (cache_c_k_l3, cache_c_v_l3)]

    cond = jnp.zeros((MOD_ROWS, d), F32).at[0].set(c_ctx).at[1:1 + dec_batch].set(c)
    mod = _modulation(cond, w_ada, b_ada).reshape(DEPTH * MOD_ROWS * 6, 1, d)
    rope = _rope_lane_tables(dec_seq)

    w_in_even_b = w_in_even.astype(BF16)
    w_out_even_b = w_out_even.astype(BF16)
    w_in_odd_b = w_in_odd.astype(BF16)
    w_out_odd_b = w_out_odd.astype(BF16)
    w_up_b = w_up.astype(BF16)
    w_down_b = w_down.astype(BF16)
    cb = ffn_conv_b[:, None, :]
    qg = jnp.tile(q_norm_g, (1, LANES // HEAD_DIM))
    kg = jnp.tile(k_norm_g, (1, LANES // HEAD_DIM))
    conv_b = b_conv_b[:, None, :]
    norm_g = b_norm_g[:, None, :]
    norm_b = b_norm_b[:, None, :]
    sub_g = subln_g[:, :, None]

    streams = [
        [x_prompt.reshape(batch * seq, d), seq, False],
        [x_sample.reshape(dec_batch * dec_seq, d), dec_seq, True],
    ]
    ctx_state = []
    for layer in range(DEPTH):
        for stream in streams:
            x, s, latent = stream
            kw = dict(seq_len=s, latent=latent)
            if layer % 2 == 0:
                e = layer // 2
                res = _in_even(x, mod, layer, w_in_even_b, qg[e][None], kg[e][None], rope, **kw)
                q, k2, vt, u = res[:4]
                if not latent:
                    ctx_state.append((res[4].reshape(batch, seq, A_KV_HEADS, HEAD_DIM),
                                      res[5].reshape(batch, seq, A_KV_HEADS, HEAD_DIM)))
                attn = _attn_even(q, k2, vt, caches[layer], **kw)
                conv = _conv_b(u, b_conv_w, conv_b, norm_g, norm_b, layer, seq_len=s)
                parts, w_out = [attn, conv], w_out_even_b
            else:
                res = _in_odd(x, mod, layer, w_in_odd_b, rope, **kw)
                q, k, vt = res[:3]
                if not latent:
                    ctx_state.append((res[3].reshape(batch, seq, C_HEADS, C_VDIM),
                                      res[4].reshape(batch, seq, C_HEADS, C_VDIM)))
                attn = _attn_odd(q, k, vt, caches[layer], diff_lambda, sub_g, layer, **kw)
                parts, w_out = [attn], w_out_odd_b
            stream[0] = _mix_ffn(parts, w_out, x, mod, layer, ln_g, ln_b, w_up_b, ffn_conv_w, cb, w_down_b, **kw)
    y_prompt = streams[0][0].reshape(batch, seq, d)
    y_sample = streams[1][0].reshape(dec_batch, dec_seq, d)
    (a_k0, a_v0), (c_k1, c_v1), (a_k2, a_v2), (c_k3, c_v3) = ctx_state
    return (y_prompt, y_sample, a_k0, a_v0, c_k1, c_v1, a_k2, a_v2, c_k3, c_v3)
```

```python
import functools
import math

import jax
import jax.numpy as jnp
from jax import lax
from jax.experimental import pallas as pl
from jax.experimental.pallas import tpu as pltpu

F32 = jnp.float32
BF16 = jnp.bfloat16

D_MODEL = 1024
DEPTH = 4
GRID_W = 64
HEAD_DIM = 64
A_HEADS = 8
A_KV_HEADS = 2
A_WIDTH = A_HEADS * HEAD_DIM
A_KV_WIDTH = A_KV_HEADS * HEAD_DIM
B_WIDTH = D_MODEL - A_WIDTH
B_CONV_W = 31
C_HEADS = D_MODEL // (2 * HEAD_DIM)
C_VDIM = 2 * HEAD_DIM
C_WIDTH = C_HEADS * C_VDIM
D_FF = 2816
ROPE_THETA = 10000.0
EPS = 1e-6
ATTN_SCALE = HEAD_DIM ** -0.5
LOG2E = math.log2(math.e)
Q_SCALE = ATTN_SCALE * LOG2E
ALPHA = (2 * DEPTH) ** 0.25
EVEN_IN = A_WIDTH + 2 * A_KV_WIDTH + 2 * B_WIDTH
ODD_IN = 3 * C_WIDTH

LANES = 128
SUBLANES = 8
BF16_ROWS = 16
MXU_DIM = 256
FF_CHUNK = MXU_DIM
FF_CHUNKS = D_FF // FF_CHUNK
MOD_ROWS = 16
QUERY_TILE = 256
PROJ_SUB = 256
PROJ_TILE = 1024
FFN_CONTEXT_SEQS = 2
NORM_BLOCK = 256
VMEM_LIMIT = 56 * 1024 * 1024


def _params(*sem):
    return pltpu.CompilerParams(dimension_semantics=sem, vmem_limit_bytes=VMEM_LIMIT)


def _dot(a, b):
    return jnp.dot(a, b, preferred_element_type=F32)


def _dot_nt(a, b):
    return lax.dot_general(a, b, (((1,), (1,)), ((), ())), preferred_element_type=F32)


def _sigmoid(x):
    return 1.0 / (1.0 + jnp.exp2(x * (-LOG2E)))


def _layer_norm(y, g, b):
    mu = jnp.mean(y, axis=-1, keepdims=True)
    d = y - mu
    var = jnp.mean(d * d, axis=-1, keepdims=True)
    return d * lax.rsqrt(var + EPS) * g + b


def _group_mean_sq(x, group):
    r = lax.broadcasted_iota(jnp.int32, (LANES, LANES), 0) // group
    c = lax.broadcasted_iota(jnp.int32, (LANES, LANES), 1) // group
    ones = jnp.where(r == c, 1.0, 0.0).astype(BF16)
    sq = x * x
    hi = sq.astype(BF16)
    lo = (sq - hi.astype(F32)).astype(BF16)
    return (_dot(hi, ones) + _dot(lo, ones)) * (1.0 / group)


def _rope(x, cos, sin):
    lane = lax.broadcasted_iota(jnp.int32, x.shape, 1)
    partner = jnp.where((lane & 16) == 0, pltpu.roll(x, LANES - 16, 1), pltpu.roll(x, 16, 1))
    return x * cos + partner * sin


def _layer_spec(a, layer, single_buffer=False):
    mode = dict(pipeline_mode=pl.Buffered(1)) if single_buffer else {}
    return pl.BlockSpec((None,) + a.shape[1:], lambda *_: (layer,) + (0,) * (a.ndim - 1), **mode)


def _mod_kernel(c_ref, w_ref, b_ref, o_ref):
    c = c_ref[...]
    s = (c * _sigmoid(c)).astype(BF16)
    o_ref[...] = _dot(s, w_ref[...].astype(BF16)) + b_ref[...]


def _modulation(cond, w_ada, b_ada):
    tn = 1536
    n = 6 * D_MODEL
    return pl.pallas_call(
        _mod_kernel,
        grid=(DEPTH, n // tn),
        in_specs=[
            pl.BlockSpec((MOD_ROWS, D_MODEL), lambda l, j: (0, 0)),
            pl.BlockSpec((None, D_MODEL, tn), lambda l, j: (l, 0, j)),
            pl.BlockSpec((None, 1, tn), lambda l, j: (l, 0, j)),
        ],
        out_specs=pl.BlockSpec((None, MOD_ROWS, tn), lambda l, j: (l, 0, j)),
        out_shape=jax.ShapeDtypeStruct((DEPTH, MOD_ROWS, n), F32),
        compiler_params=_params("arbitrary", "arbitrary"), name="adaln_mod",
    )(cond, w_ada, b_ada.reshape(DEPTH, 1, n))


def _mod_spec(layer, which, tiles_per_seq, latent):
    def index(i, *_):
        row = 1 + i // tiles_per_seq if latent else 0
        return ((layer * MOD_ROWS + row) * 6 + which, 0, 0)
    return pl.BlockSpec((None, 1, D_MODEL), index)


def _sub_tile_projections(x_ref, sh_ref, sc_ref, w_ref):
    h = (x_ref[...] * (1.0 + sc_ref[...]) + sh_ref[...]).astype(BF16)
    subs = [slice(r, r + PROJ_SUB) for r in range(0, h.shape[0], PROJ_SUB)]
    return list(zip(subs, [_dot(h[rows], w_ref[...]) for rows in subs]))

def _in_even_kernel(*refs, latent):
    if latent:
        x_ref, sh_ref, sc_ref, w_ref, qg_ref, kg_ref, cos_ref, sin_ref, q_ref, k_ref, v_ref, u_ref = refs
    else:
        x_ref, sh_ref, sc_ref, w_ref, qg_ref, kg_ref, q_ref, k_ref, v_ref, u_ref, ko_ref, vo_ref = refs
    for rows, proj in _sub_tile_projections(x_ref, sh_ref, sc_ref, w_ref):
        for c in range((A_WIDTH + A_KV_WIDTH) // LANES):
            lo = c * LANES
            x = proj[:, lo:lo + LANES]
            is_q = lo < A_WIDTH
            gain = qg_ref[...] if is_q else kg_ref[...]
            x = x * lax.rsqrt(_group_mean_sq(x, HEAD_DIM) + EPS) * gain
            if not latent and not is_q:
                ko_ref[rows, :] = x
            if latent:
                x = _rope(x, cos_ref[rows, :], sin_ref[rows, :])
            if is_q:
                q_ref[rows, lo:lo + LANES] = (x * Q_SCALE).astype(BF16)
            else:
                k_ref[rows, 0:LANES] = x.astype(BF16)
                k_ref[rows, LANES:2 * LANES] = pltpu.roll(x, HEAD_DIM, 1).astype(BF16)
        v = proj[:, A_WIDTH + A_KV_WIDTH:A_WIDTH + 2 * A_KV_WIDTH]
        v_ref[:, rows] = v.T.astype(BF16)
        if not latent:
            vo_ref[rows, :] = v
        u0 = A_WIDTH + 2 * A_KV_WIDTH
        ua = proj[:, u0:u0 + B_WIDTH]
        ub = proj[:, u0 + B_WIDTH:u0 + 2 * B_WIDTH]
        u_ref[rows, :] = ua * _sigmoid(ub)


def _in_even(x, mod, layer, w_in, qg, kg, rope, *, seq_len, latent):
    rows = x.shape[0]
    tm = PROJ_TILE
    tps = max(seq_len // tm, 1)
    row_spec = lambda w: pl.BlockSpec((tm, w), lambda i: (i, 0))
    full = lambda a: pl.BlockSpec(a.shape, lambda i: (0,) * a.ndim)
    in_specs = [row_spec(D_MODEL), _mod_spec(layer, 0, tps, latent), _mod_spec(layer, 1, tps, latent),
                _layer_spec(w_in, layer // 2), full(qg), full(kg)]
    args = [x, mod, mod, w_in, qg, kg]
    out_specs = [row_spec(A_WIDTH), row_spec(2 * A_KV_WIDTH),
                 pl.BlockSpec((A_KV_WIDTH, tm), lambda i: (0, i)), row_spec(B_WIDTH)]
    out_shape = [jax.ShapeDtypeStruct((rows, A_WIDTH), BF16), jax.ShapeDtypeStruct((rows, 2 * A_KV_WIDTH), BF16),
                 jax.ShapeDtypeStruct((A_KV_WIDTH, rows), BF16), jax.ShapeDtypeStruct((rows, B_WIDTH), F32)]
    if latent:
        pos_spec = pl.BlockSpec((tm, LANES), lambda i: (i % tps, 0))
        in_specs += [pos_spec, pos_spec]
        args += list(rope)
    else:
        out_specs += [row_spec(A_KV_WIDTH), row_spec(A_KV_WIDTH)]
        out_shape += [jax.ShapeDtypeStruct((rows, A_KV_WIDTH), F32)] * 2
    return pl.pallas_call(
        functools.partial(_in_even_kernel, latent=latent),
        grid=(rows // tm,), in_specs=in_specs, out_specs=out_specs, out_shape=out_shape,
        compiler_params=_params("arbitrary"), name="in_even",
    )(*args)


def _in_odd_kernel(*refs, latent):
    if latent:
        x_ref, sh_ref, sc_ref, w_ref, cos_ref, sin_ref, q_ref, k_ref, v_ref = refs
    else:
        x_ref, sh_ref, sc_ref, w_ref, q_ref, k_ref, v_ref, ko_ref, vo_ref = refs
    for rows, proj in _sub_tile_projections(x_ref, sh_ref, sc_ref, w_ref):
        for c in range(2 * C_WIDTH // LANES):
            lo = c * LANES
            x = proj[:, lo:lo + LANES]
            if latent:
                x = _rope(x, cos_ref[rows, :], sin_ref[rows, :])
            if lo < C_WIDTH:
                q_ref[rows, lo:lo + LANES] = (x * Q_SCALE).astype(BF16)
            else:
                k_ref[rows, lo - C_WIDTH:lo - C_WIDTH + LANES] = x.astype(BF16)
        v = proj[:, 2 * C_WIDTH:]
        v_ref[:, rows] = v.T.astype(BF16)
        if not latent:
            ko_ref[rows, :] = proj[:, C_WIDTH:2 * C_WIDTH]
            vo_ref[rows, :] = v


def _in_odd(x, mod, layer, w_in, rope, *, seq_len, latent):
    rows = x.shape[0]
    tm = PROJ_TILE
    tps = max(seq_len // tm, 1)
    row_spec = lambda w: pl.BlockSpec((tm, w), lambda i: (i, 0))
    in_specs = [row_spec(D_MODEL), _mod_spec(layer, 0, tps, latent), _mod_spec(layer, 1, tps, latent),
                _layer_spec(w_in, layer // 2)]
    args = [x, mod, mod, w_in]
    out_specs = [row_spec(C_WIDTH)] * 2 + [pl.BlockSpec((C_WIDTH, tm), lambda i: (0, i))]
    out_shape = [jax.ShapeDtypeStruct((rows, C_WIDTH), BF16)] * 2 + [jax.ShapeDtypeStruct((C_WIDTH, rows), BF16)]
    if latent:
        pos_spec = pl.BlockSpec((tm, LANES), lambda i: (i % tps, 0))
        in_specs += [pos_spec, pos_spec]
        args += list(rope)
    else:
        out_specs += [row_spec(C_WIDTH)] * 2
        out_shape += [jax.ShapeDtypeStruct((rows, C_WIDTH), F32)] * 2
    return pl.pallas_call(
        functools.partial(_in_odd_kernel, latent=latent),
        grid=(rows // tm,), in_specs=in_specs, out_specs=out_specs, out_shape=out_shape,
        compiler_params=_params("arbitrary"), name="in_odd",
    )(*args)


CONV_HALO = 16


def _conv_b_kernel(u_ref, w_ref, b_ref, g_ref, beta_ref, o_ref, pad_ref, *, seq_len):
    chunk = 128
    zeros = jnp.zeros((CONV_HALO, B_WIDTH), F32)
    pad_ref[0:CONV_HALO, :] = zeros
    pad_ref[CONV_HALO + seq_len:2 * CONV_HALO + seq_len, :] = zeros
    pad_ref[CONV_HALO:CONV_HALO + seq_len, :] = u_ref[...]
    centre = B_CONV_W // 2
    for r in range(seq_len // chunk):
        base = CONV_HALO + r * chunk
        cols = []
        for c in range(B_WIDTH // LANES):
            lanes = slice(c * LANES, (c + 1) * LANES)
            acc = None
            for b in range(SUBLANES):
                part = None
                for a in range(-2, 2):
                    k = SUBLANES * a + b + centre
                    if not 0 <= k < B_CONV_W:
                        continue
                    start = base + SUBLANES * a
                    term = pad_ref[start:start + chunk + SUBLANES, lanes] * w_ref[k:k + 1, lanes]
                    part = term if part is None else part + term
                part = part[b:b + chunk]
                acc = part if acc is None else acc + part
            cols.append(acc + b_ref[:, lanes])
        y = _layer_norm(jnp.concatenate(cols, axis=-1), g_ref[...], beta_ref[...])
        o_ref[r * chunk:(r + 1) * chunk, :] = (y * _sigmoid(y)).astype(BF16)


def _conv_b(u, w, b, g, beta, layer, *, seq_len):
    rows = u.shape[0]
    e = layer // 2
    return pl.pallas_call(
        functools.partial(_conv_b_kernel, seq_len=seq_len),
        grid=(rows // seq_len,),
        in_specs=[pl.BlockSpec((seq_len, B_WIDTH), lambda i: (i, 0)), _layer_spec(w, e),
                  _layer_spec(b, e), _layer_spec(g, e), _layer_spec(beta, e)],
        out_specs=pl.BlockSpec((seq_len, B_WIDTH), lambda i: (i, 0)),
        out_shape=jax.ShapeDtypeStruct((rows, B_WIDTH), BF16),
        scratch_shapes=[pltpu.VMEM((seq_len + 2 * CONV_HALO, B_WIDTH), F32)],
        compiler_params=_params("arbitrary"), name="conv_b",
    )(u, w, b, g, beta)


KEY_CHUNK = MXU_DIM
SCORE_LOOKAHEAD = 8


def _half_lanes(x, half):
    lane = lax.broadcasted_iota(jnp.int32, x.shape, 1)
    return jnp.where((lane < HEAD_DIM) == (half == 0), x, jnp.zeros_like(x))


def _attend(n_items, n_keys, v_dim, query, key_chunk, value_chunk):
    n_chunks = n_keys // KEY_CHUNK
    steps = [(i, c) for i in range(n_items) for c in range(n_chunks)]
    queries = {}

    def scores(step):
        i, c = step
        if i not in queries:
            queries[i] = query(i)
        return _dot_nt(key_chunk(i, slice(c * KEY_CHUNK, (c + 1) * KEY_CHUNK)), queries[i])

    pending = [scores(st) for st in steps[:SCORE_LOOKAHEAD]]
    outs = []
    for n, (i, c) in enumerate(steps):
        s = pending.pop(0)
        if n + SCORE_LOOKAHEAD < len(steps):
            pending.append(scores(steps[n + SCORE_LOOKAHEAD]))
        vt = value_chunk(i, slice(c * KEY_CHUNK, (c + 1) * KEY_CHUNK))
        m = s.max(axis=0, keepdims=True)
        if c == 0:
            o = _dot(vt, jnp.exp2(s - m).astype(BF16))
        else:
            m = jnp.maximum(m_run, m)
            o = o_run * jnp.exp2(m_run - m) + _dot(vt, jnp.exp2(s - m).astype(BF16))
        m_run, o_run = m, o
        if c == n_chunks - 1:
            queries.pop(i)
            outs.append(o[0:v_dim] * (1.0 / o[v_dim:v_dim + 1]))
    return outs


A_VT_ROWS = HEAD_DIM + BF16_ROWS
C_VT_ROWS = C_VDIM + BF16_ROWS


def _attn_even_kernel(*refs, latent, seq_len):
    if latent:
        q_ref, k_ref, vt_ref, ck_ref, cv_ref, o_ref, kall_ref, vtall_ref = refs
    else:
        q_ref, k_ref, vt_ref, o_ref, vtall_ref = refs
        kall_ref = k_ref
    n_keys = vtall_ref.shape[1]

    @pl.when(pl.program_id(1) == 0)
    def _():
        if latent:
            kall_ref[0:seq_len, :] = k_ref[...]
            ck = ck_ref[...]
            kall_ref[seq_len:, 0:LANES] = ck.astype(BF16)
            kall_ref[seq_len:, LANES:2 * LANES] = pltpu.roll(ck, HEAD_DIM, 1).astype(BF16)
            cvt = cv_ref[...].T
        for j in range(A_KV_HEADS):
            r0 = j * A_VT_ROWS
            vtall_ref[r0:r0 + HEAD_DIM, 0:seq_len] = vt_ref[j * HEAD_DIM:(j + 1) * HEAD_DIM, :]
            if latent:
                vtall_ref[r0:r0 + HEAD_DIM, seq_len:] = cvt[j * HEAD_DIM:(j + 1) * HEAD_DIM].astype(BF16)
            vtall_ref[r0 + HEAD_DIM:r0 + A_VT_ROWS, :] = jnp.ones((BF16_ROWS, n_keys), BF16)

    group = A_HEADS // A_KV_HEADS

    def query(h):
        lo = (h // 2) * LANES
        return _half_lanes(q_ref[:, lo:lo + LANES], h % 2)

    def key_chunk(h, rows):
        order = 0 if h % 2 == h // group else 1
        return kall_ref[rows, order * LANES:(order + 1) * LANES]

    def value_chunk(h, cols):
        r0 = (h // group) * A_VT_ROWS
        return vtall_ref[r0:r0 + A_VT_ROWS, cols]

    outs = _attend(A_HEADS, n_keys, HEAD_DIM, query, key_chunk, value_chunk)
    o_ref[...] = jnp.concatenate(outs, axis=0).T.astype(BF16)


def _attn_even(q, k2, vt, cache, *, seq_len, latent):
    rows = q.shape[0]
    tq = QUERY_TILE
    tps = seq_len // tq
    q_spec = pl.BlockSpec((tq, A_WIDTH), lambda b, i: (b * tps + i, 0))
    in_specs = [q_spec, pl.BlockSpec((seq_len, 2 * A_KV_WIDTH), lambda b, i: (b, 0)),
                pl.BlockSpec((A_KV_WIDTH, seq_len), lambda b, i: (0, b))]
    args = [q, k2, vt]
    n_keys = seq_len
    scratch = []
    if latent:
        past = cache[0].shape[1]
        n_keys += past
        c_spec = pl.BlockSpec((None, past, A_KV_WIDTH), lambda b, i: (b, 0, 0))
        in_specs += [c_spec, c_spec]
        args += [cache[0].reshape(-1, past, A_KV_WIDTH), cache[1].reshape(-1, past, A_KV_WIDTH)]
        scratch.append(pltpu.VMEM((n_keys, 2 * A_KV_WIDTH), BF16))
    scratch.append(pltpu.VMEM((A_KV_HEADS * A_VT_ROWS, n_keys), BF16))
    return pl.pallas_call(
        functools.partial(_attn_even_kernel, latent=latent, seq_len=seq_len),
        grid=(rows // seq_len, tps), in_specs=in_specs, out_specs=q_spec,
        out_shape=jax.ShapeDtypeStruct((rows, A_WIDTH), BF16), scratch_shapes=scratch,
        compiler_params=_params("arbitrary", "arbitrary"), name="attn_even",
    )(*args)


def _attn_odd_kernel(*refs, latent, lam_init, seq_len):
    if latent:
        q_ref, k_ref, vt_ref, ck_ref, cv_ref, dl_ref, sg_ref, o_ref, kall_ref, vtall_ref = refs
    else:
        q_ref, k_ref, vt_ref, dl_ref, sg_ref, o_ref, vtall_ref = refs
        kall_ref = k_ref
    n_keys = vtall_ref.shape[1]
    past = n_keys - seq_len

    @pl.when(pl.program_id(1) == 0)
    def _():
        if latent:
            kall_ref[0:seq_len, :] = k_ref[...]
        for h in range(C_HEADS):
            lanes = slice(h * C_VDIM, (h + 1) * C_VDIM)
            r0 = h * C_VT_ROWS
            vtall_ref[r0:r0 + C_VDIM, 0:seq_len] = vt_ref[lanes, :]
            if latent:
                kall_ref[seq_len:, lanes] = ck_ref[pl.ds(h, past, stride=C_HEADS), :].astype(BF16)
                vtall_ref[r0:r0 + C_VDIM, seq_len:] = cv_ref[pl.ds(h, past, stride=C_HEADS), :].T.astype(BF16)
            vtall_ref[r0 + C_VDIM:r0 + C_VT_ROWS, :] = jnp.ones((BF16_ROWS, n_keys), BF16)

    dl = dl_ref[...]
    lam = (jnp.exp(jnp.sum(dl[0:1] * dl[1:2], axis=-1, keepdims=True))
           - jnp.exp(jnp.sum(dl[2:3] * dl[3:4], axis=-1, keepdims=True)) + lam_init)

    def lanes_of(i):
        return slice((i // 2) * C_VDIM, (i // 2 + 1) * C_VDIM)

    def query(i):
        return _half_lanes(q_ref[:, lanes_of(i)], i % 2)

    def key_chunk(i, rows):
        return kall_ref[rows, lanes_of(i)]

    def value_chunk(i, cols):
        r0 = (i // 2) * C_VT_ROWS
        return vtall_ref[r0:r0 + C_VT_ROWS, cols]

    outs = _attend(2 * C_HEADS, n_keys, C_VDIM, query, key_chunk, value_chunk)
    for h in range(C_HEADS):
        o = outs[2 * h] - lam * outs[2 * h + 1]
        ms = jnp.mean(o * o, axis=0, keepdims=True)
        o = o * lax.rsqrt(ms + EPS) * sg_ref[...] * (1.0 - lam_init)
        o_ref[:, h * C_VDIM:(h + 1) * C_VDIM] = o.T.astype(BF16)


def _attn_odd(q, k, vt, cache, dl, sg, layer, *, seq_len, latent):
    rows = q.shape[0]
    tq = QUERY_TILE
    tps = seq_len // tq
    lam_init = 0.8 - 0.6 * math.exp(-0.3 * layer)
    q_spec = pl.BlockSpec((tq, C_WIDTH), lambda b, i: (b * tps + i, 0))
    in_specs = [q_spec, pl.BlockSpec((seq_len, C_WIDTH), lambda b, i: (b, 0)),
                pl.BlockSpec((C_WIDTH, seq_len), lambda b, i: (0, b))]
    args = [q, k, vt]
    n_keys = seq_len
    scratch = []
    if latent:
        past = cache[0].shape[1]
        n_keys += past
        c_spec = pl.BlockSpec((None, past * C_HEADS, C_VDIM), lambda b, i: (b, 0, 0))
        in_specs += [c_spec, c_spec]
        args += [cache[0].reshape(-1, past * C_HEADS, C_VDIM), cache[1].reshape(-1, past * C_HEADS, C_VDIM)]
        scratch.append(pltpu.VMEM((n_keys, C_WIDTH), BF16))
    scratch.append(pltpu.VMEM((C_HEADS * C_VT_ROWS, n_keys), BF16))
    in_specs += [_layer_spec(dl, layer // 2), _layer_spec(sg, layer // 2)]
    args += [dl, sg]
    return pl.pallas_call(
        functools.partial(_attn_odd_kernel, latent=latent, lam_init=lam_init, seq_len=seq_len),
        grid=(rows // seq_len, tps), in_specs=in_specs, out_specs=q_spec,
        out_shape=jax.ShapeDtypeStruct((rows, C_WIDTH), BF16), scratch_shapes=scratch,
        compiler_params=_params("arbitrary", "arbitrary"), name="attn_odd",
    )(*args)


def _neighbour_rows(u):
    n = u.shape[0]
    sub = lax.broadcasted_iota(jnp.int32, (SUBLANES, u.shape[1]), 0)
    prev = pltpu.roll(u, 1, 0)
    nxt = pltpu.roll(u, n - 1, 0)
    prev = jnp.concatenate([jnp.where(sub == 0, 0.0, prev[0:SUBLANES]), prev[SUBLANES:]], axis=0)
    nxt = jnp.concatenate([nxt[:n - SUBLANES], jnp.where(sub == SUBLANES - 1, 0.0, nxt[n - SUBLANES:])], axis=0)
    return prev, nxt


def _mix_ffn_kernel(*refs, n_parts, seq_len):
    parts = refs[:n_parts]
    (wo_ref, x_ref, g1_ref, sh_ref, sc_ref, g2_ref, lng_ref, lnb_ref, wu_ref, cw_ref, cb_ref, wd_ref,
     o_ref, x1_ref, a_ref) = refs[n_parts:]
    width = wo_ref.shape[0] // n_parts
    spans = [slice(r, r + seq_len) for r in range(0, x_ref.shape[0], seq_len)]

    def conv(u, lo):
        cols = slice(lo, lo + FF_CHUNK)
        prev, nxt = _neighbour_rows(u)
        return prev * cw_ref[0:1, cols] + u * cw_ref[1:2, cols] + nxt * cw_ref[2:3, cols] + cb_ref[:, cols]

    def blocks(rows):
        return [slice(r, r + NORM_BLOCK) for r in range(rows.start, rows.stop, NORM_BLOCK)]

    hs = []
    for rows in spans:
        for blk in blocks(rows):
            mix = _dot(parts[0][blk, :], wo_ref[0:width, :])
            for j in range(1, n_parts):
                mix = mix + _dot(parts[j][blk, :], wo_ref[j * width:(j + 1) * width, :])
            x1_ref[blk, :] = _layer_norm(ALPHA * x_ref[blk, :] + g1_ref[...] * mix, lng_ref[0:1, :], lnb_ref[0:1, :])
        hs.append((x1_ref[rows, :] * (1.0 + sc_ref[...]) + sh_ref[...]).astype(BF16))
    for rows, h in zip(spans, hs):
        for j in range(FF_CHUNKS):
            lo = j * FF_CHUNK
            gate = conv(_dot(h, wu_ref[:, lo:lo + FF_CHUNK]), lo)
            val = conv(_dot(h, wu_ref[:, D_FF + lo:D_FF + lo + FF_CHUNK]), D_FF + lo)
            a_ref[rows, lo:lo + FF_CHUNK] = (gate * _sigmoid(gate) * val).astype(BF16)
        for blk in blocks(rows):
            y = ALPHA * x1_ref[blk, :] + g2_ref[...] * _dot(a_ref[blk, :], wd_ref[...])
            o_ref[blk, :] = _layer_norm(y, lng_ref[1:2, :], lnb_ref[1:2, :])


def _mix_ffn(parts, w_out, x, mod, layer, ln_g, ln_b, w_up, cw, cb, w_down, *, seq_len, latent):
    rows = x.shape[0]
    tm = seq_len if latent else FFN_CONTEXT_SEQS * seq_len
    row_spec = lambda w: pl.BlockSpec((tm, w), lambda i: (i, 0))
    res = lambda a, l: _layer_spec(a, l, single_buffer=True)
    in_specs = [row_spec(p.shape[1]) for p in parts]
    in_specs += [res(w_out, layer // 2), row_spec(D_MODEL), _mod_spec(layer, 2, 1, latent),
                 _mod_spec(layer, 3, 1, latent), _mod_spec(layer, 4, 1, latent), _mod_spec(layer, 5, 1, latent),
                 res(ln_g, layer), res(ln_b, layer), res(w_up, layer), res(cw, layer), res(cb, layer),
                 res(w_down, layer)]
    return pl.pallas_call(
        functools.partial(_mix_ffn_kernel, n_parts=len(parts), seq_len=seq_len),
        grid=(rows // tm,), in_specs=in_specs, out_specs=row_spec(D_MODEL),
        out_shape=jax.ShapeDtypeStruct((rows, D_MODEL), F32),
        scratch_shapes=[pltpu.VMEM((tm, D_MODEL), F32), pltpu.VMEM((tm, D_FF), BF16)],
        compiler_params=_params("arbitrary"), name="mix_ffn",
    )(*parts, w_out, x, mod, mod, mod, mod, ln_g, ln_b, w_up, cw, cb, w_down)


def _rope_lane_tables(n):
    t = jnp.arange(n)
    half = HEAD_DIM // 2
    inv = 1.0 / (ROPE_THETA ** (jnp.arange(0, half, 2, dtype=F32) / half))
    ang_r = (t // GRID_W).astype(F32)[:, None] * inv
    ang_c = (t % GRID_W).astype(F32)[:, None] * inv
    cos = jnp.concatenate([jnp.cos(ang_r)] * 2 + [jnp.cos(ang_c)] * 2, axis=-1)
    sin = jnp.concatenate([-jnp.sin(ang_r), jnp.sin(ang_r), -jnp.sin(ang_c), jnp.sin(ang_c)], axis=-1)
    reps = LANES // HEAD_DIM
    return jnp.tile(cos, (1, reps)), jnp.tile(sin, (1, reps))


def kernel(x_prompt, x_sample, cache_a_k_l0, cache_a_v_l0, cache_c_k_l1, cache_c_v_l1, cache_a_k_l2,
           cache_a_v_l2, cache_c_k_l3, cache_c_v_l3, c, c_ctx, w_ada, b_ada, ln_g, ln_b, w_in_even,
           q_norm_g, k_norm_g, b_conv_w, b_conv_b, b_norm_g, b_norm_b, w_out_even, w_in_odd,
           diff_lambda, subln_g, w_out_odd, w_up, ffn_conv_w, ffn_conv_b, w_down):
    batch, seq, d = x_prompt.shape
    dec_batch, dec_seq, _ = x_sample.shape
    assert d == D_MODEL and 1 + dec_batch <= MOD_ROWS
    caches = [(cache_a_k_l0, cache_a_v_l0), (cache_c_k_l1, cache_c_v_l1),
              (cache_a_k_l2, cache_a_v_l2), (cache_c_k_l3, cache_c_v_l3)]

    cond = jnp.zeros((MOD_ROWS, d), F32).at[0].set(c_ctx).at[1:1 + dec_batch].set(c)
    mod = _modulation(cond, w_ada, b_ada).reshape(DEPTH * MOD_ROWS * 6, 1, d)
    rope = _rope_lane_tables(dec_seq)

    w_in_even_b = w_in_even.astype(BF16)
    w_out_even_b = w_out_even.astype(BF16)
    w_in_odd_b = w_in_odd.astype(BF16)
    w_out_odd_b = w_out_odd.astype(BF16)
    w_up_b = w_up.astype(BF16)
    w_down_b = w_down.astype(BF16)
    cb = ffn_conv_b[:, None, :]
    qg = jnp.tile(q_norm_g, (1, LANES // HEAD_DIM))
    kg = jnp.tile(k_norm_g, (1, LANES // HEAD_DIM))
    conv_b = b_conv_b[:, None, :]
    norm_g = b_norm_g[:, None, :]
    norm_b = b_norm_b[:, None, :]
    sub_g = subln_g[:, :, None]

    streams = [
        [x_prompt.reshape(batch * seq, d), seq, False],
        [x_sample.reshape(dec_batch * dec_seq, d), dec_seq, True],
    ]
    ctx_state = []
    for layer in range(DEPTH):
        for stream in streams:
            x, s, latent = stream
            kw = dict(seq_len=s, latent=latent)
            if layer % 2 == 0:
                e = layer // 2
                res = _in_even(x, mod, layer, w_in_even_b, qg[e][None], kg[e][None], rope, **kw)
                q, k2, vt, u = res[:4]
                if not latent:
                    ctx_state.append((res[4].reshape(batch, seq, A_KV_HEADS, HEAD_DIM),
                                      res[5].reshape(batch, seq, A_KV_HEADS, HEAD_DIM)))
                attn = _attn_even(q, k2, vt, caches[layer], **kw)
                conv = _conv_b(u, b_conv_w, conv_b, norm_g, norm_b, layer, seq_len=s)
                parts, w_out = [attn, conv], w_out_even_b
            else:
                res = _in_odd(x, mod, layer, w_in_odd_b, rope, **kw)
                q, k, vt = res[:3]
                if not latent:
                    ctx_state.append((res[3].reshape(batch, seq, C_HEADS, C_VDIM),
                                      res[4].reshape(batch, seq, C_HEADS, C_VDIM)))
                attn = _attn_odd(q, k, vt, caches[layer], diff_lambda, sub_g, layer, **kw)
                parts, w_out = [attn], w_out_odd_b
            stream[0] = _mix_ffn(parts, w_out, x, mod, layer, ln_g, ln_b, w_up_b, ffn_conv_w, cb, w_down_b, **kw)
    y_prompt = streams[0][0].reshape(batch, seq, d)
    y_sample = streams[1][0].reshape(dec_batch, dec_seq, d)
    (a_k0, a_v0), (c_k1, c_v1), (a_k2, a_v2), (c_k3, c_v3) = ctx_state
    return (y_prompt, y_sample, a_k0, a_v0, c_k1, c_v1, a_k2, a_v2, c_k3, c_v3)
```

```python
import functools
import math

import jax
import jax.numpy as jnp
from jax import lax
from jax.experimental import pallas as pl
from jax.experimental.pallas import tpu as pltpu

F32 = jnp.float32
BF16 = jnp.bfloat16

D_MODEL = 1024
DEPTH = 4
GRID_W = 64
HEAD_DIM = 64
A_HEADS = 8
A_KV_HEADS = 2
A_WIDTH = A_HEADS * HEAD_DIM
A_KV_WIDTH = A_KV_HEADS * HEAD_DIM
B_WIDTH = D_MODEL - A_WIDTH
B_CONV_W = 31
C_HEADS = D_MODEL // (2 * HEAD_DIM)
C_VDIM = 2 * HEAD_DIM
C_WIDTH = C_HEADS * C_VDIM
D_FF = 2816
ROPE_THETA = 10000.0
EPS = 1e-6
ATTN_SCALE = HEAD_DIM ** -0.5
LOG2E = math.log2(math.e)
Q_SCALE = ATTN_SCALE * LOG2E
ALPHA = (2 * DEPTH) ** 0.25
EVEN_IN = A_WIDTH + 2 * A_KV_WIDTH + 2 * B_WIDTH
ODD_IN = 3 * C_WIDTH

LANES = 128
SUBLANES = 8
BF16_ROWS = 16
MXU_DIM = 256
FF_CHUNK = MXU_DIM
FF_CHUNKS = D_FF // FF_CHUNK
MOD_ROWS = 16
QUERY_TILE = 256
PROJ_SUB = 256
PROJ_TILE = 1024
FFN_CONTEXT_SEQS = 2
NORM_BLOCK = 256
VMEM_LIMIT = 56 * 1024 * 1024


def _params(*sem):
    return pltpu.CompilerParams(dimension_semantics=sem, vmem_limit_bytes=VMEM_LIMIT)


def _dot(a, b):
    return jnp.dot(a, b, preferred_element_type=F32)


def _dot_nt(a, b):
    return lax.dot_general(a, b, (((1,), (1,)), ((), ())), preferred_element_type=F32)


def _sigmoid(x):
    return 1.0 / (1.0 + jnp.exp2(x * (-LOG2E)))


def _layer_norm(y, g, b):
    mu = jnp.mean(y, axis=-1, keepdims=True)
    d = y - mu
    var = jnp.mean(d * d, axis=-1, keepdims=True)
    return d * lax.rsqrt(var + EPS) * g + b


def _group_mean_sq(x, group):
    r = lax.broadcasted_iota(jnp.int32, (LANES, LANES), 0) // group
    c = lax.broadcasted_iota(jnp.int32, (LANES, LANES), 1) // group
    ones = jnp.where(r == c, 1.0, 0.0).astype(BF16)
    sq = x * x
    hi = sq.astype(BF16)
    lo = (sq - hi.astype(F32)).astype(BF16)
    return (_dot(hi, ones) + _dot(lo, ones)) * (1.0 / group)


def _rope(x, cos, sin):
    lane = lax.broadcasted_iota(jnp.int32, x.shape, 1)
    partner = jnp.where((lane & 16) == 0, pltpu.roll(x, LANES - 16, 1), pltpu.roll(x, 16, 1))
    return x * cos + partner * sin


def _layer_spec(a, layer, single_buffer=False):
    mode = dict(pipeline_mode=pl.Buffered(1)) if single_buffer else {}
    return pl.BlockSpec((None,) + a.shape[1:], lambda *_: (layer,) + (0,) * (a.ndim - 1), **mode)


def _mod_kernel(c_ref, w_ref, b_ref, o_ref):
    c = c_ref[...]
    s = (c * _sigmoid(c)).astype(BF16)
    o_ref[...] = _dot(s, w_ref[...].astype(BF16)) + b_ref[...]


def _modulation(cond, w_ada, b_ada):
    tn = 1536
    n = 6 * D_MODEL
    return pl.pallas_call(
        _mod_kernel,
        grid=(DEPTH, n // tn),
        in_specs=[
            pl.BlockSpec((MOD_ROWS, D_MODEL), lambda l, j: (0, 0)),
            pl.BlockSpec((None, D_MODEL, tn), lambda l, j: (l, 0, j)),
            pl.BlockSpec((None, 1, tn), lambda l, j: (l, 0, j)),
        ],
        out_specs=pl.BlockSpec((None, MOD_ROWS, tn), lambda l, j: (l, 0, j)),
        out_shape=jax.ShapeDtypeStruct((DEPTH, MOD_ROWS, n), F32),
        compiler_params=_params("arbitrary", "arbitrary"), name="adaln_mod",
    )(cond, w_ada, b_ada.reshape(DEPTH, 1, n))


def _mod_spec(layer, which, tiles_per_seq, latent):
    def index(i, *_):
        row = 1 + i // tiles_per_seq if latent else 0
        return ((layer * MOD_ROWS + row) * 6 + which, 0, 0)
    return pl.BlockSpec((None, 1, D_MODEL), index)


def _sub_tile_projections(x_ref, sh_ref, sc_ref, w_ref):
    h = (x_ref[...] * (1.0 + sc_ref[...]) + sh_ref[...]).astype(BF16)
    subs = [slice(r, r + PROJ_SUB) for r in range(0, h.shape[0], PROJ_SUB)]
    return list(zip(subs, [_dot(h[rows], w_ref[...]) for rows in subs]))

def _in_even_kernel(*refs, latent, seq_len):
    if latent:
        (x_ref, sh_ref, sc_ref, w_ref, qg_ref, kg_ref, cw_ref, cb_ref, cg_ref, cbeta_ref, cos_ref, sin_ref,
         q_ref, k_ref, v_ref, c_ref, pad_ref) = refs
    else:
        (x_ref, sh_ref, sc_ref, w_ref, qg_ref, kg_ref, cw_ref, cb_ref, cg_ref, cbeta_ref,
         q_ref, k_ref, v_ref, c_ref, ko_ref, vo_ref, pad_ref) = refs
    n_seq = pad_ref.shape[0]
    zeros = jnp.zeros((CONV_HALO, B_WIDTH), F32)
    for s in range(n_seq):
        pad_ref[s, 0:CONV_HALO, :] = zeros
        pad_ref[s, CONV_HALO + seq_len:2 * CONV_HALO + seq_len, :] = zeros
    for rows, proj in _sub_tile_projections(x_ref, sh_ref, sc_ref, w_ref):
        for c in range((A_WIDTH + A_KV_WIDTH) // LANES):
            lo = c * LANES
            x = proj[:, lo:lo + LANES]
            is_q = lo < A_WIDTH
            gain = qg_ref[...] if is_q else kg_ref[...]
            x = x * lax.rsqrt(_group_mean_sq(x, HEAD_DIM) + EPS) * gain
            if not latent and not is_q:
                ko_ref[rows, :] = x
            if latent:
                x = _rope(x, cos_ref[rows, :], sin_ref[rows, :])
            if is_q:
                q_ref[rows, lo:lo + LANES] = (x * Q_SCALE).astype(BF16)
            else:
                k_ref[rows, 0:LANES] = x.astype(BF16)
                k_ref[rows, LANES:2 * LANES] = pltpu.roll(x, HEAD_DIM, 1).astype(BF16)
        v = proj[:, A_WIDTH + A_KV_WIDTH:A_WIDTH + 2 * A_KV_WIDTH]
        v_ref[:, rows] = v.T.astype(BF16)
        if not latent:
            vo_ref[rows, :] = v
        u0 = A_WIDTH + 2 * A_KV_WIDTH
        ua = proj[:, u0:u0 + B_WIDTH]
        ub = proj[:, u0 + B_WIDTH:u0 + 2 * B_WIDTH]
        seq, off = divmod(rows.start, seq_len)
        pad_ref[seq, CONV_HALO + off:CONV_HALO + off + PROJ_SUB, :] = ua * _sigmoid(ub)
    for s in range(n_seq):
        _conv_module(pad_ref, s, seq_len, cw_ref, cb_ref, cg_ref, cbeta_ref, c_ref)


def _in_even(x, mod, layer, w_in, qg, kg, conv, rope, *, seq_len, latent):
    rows = x.shape[0]
    tm = PROJ_TILE
    assert tm % seq_len == 0 and seq_len % PROJ_SUB == 0
    tps = max(seq_len // tm, 1)
    row_spec = lambda w: pl.BlockSpec((tm, w), lambda i: (i, 0))
    full = lambda a: pl.BlockSpec(a.shape, lambda i: (0,) * a.ndim)
    in_specs = [row_spec(D_MODEL), _mod_spec(layer, 0, tps, latent), _mod_spec(layer, 1, tps, latent),
                _layer_spec(w_in, layer // 2), full(qg), full(kg)]
    in_specs += [_layer_spec(a, layer // 2) for a in conv]
    args = [x, mod, mod, w_in, qg, kg, *conv]
    out_specs = [row_spec(A_WIDTH), row_spec(2 * A_KV_WIDTH),
                 pl.BlockSpec((A_KV_WIDTH, tm), lambda i: (0, i)), row_spec(B_WIDTH)]
    out_shape = [jax.ShapeDtypeStruct((rows, A_WIDTH), BF16), jax.ShapeDtypeStruct((rows, 2 * A_KV_WIDTH), BF16),
                 jax.ShapeDtypeStruct((A_KV_WIDTH, rows), BF16), jax.ShapeDtypeStruct((rows, B_WIDTH), BF16)]
    if latent:
        pos_spec = pl.BlockSpec((tm, LANES), lambda i: (i % tps, 0))
        in_specs += [pos_spec, pos_spec]
        args += list(rope)
    else:
        out_specs += [row_spec(A_KV_WIDTH), row_spec(A_KV_WIDTH)]
        out_shape += [jax.ShapeDtypeStruct((rows, A_KV_WIDTH), F32)] * 2
    return pl.pallas_call(
        functools.partial(_in_even_kernel, latent=latent, seq_len=seq_len),
        grid=(rows // tm,), in_specs=in_specs, out_specs=out_specs, out_shape=out_shape,
        scratch_shapes=[pltpu.VMEM((tm // seq_len, seq_len + 2 * CONV_HALO, B_WIDTH), F32)],
        compiler_params=_params("arbitrary"), name="in_even",
    )(*args)


def _in_odd_kernel(*refs, latent):
    if latent:
        x_ref, sh_ref, sc_ref, w_ref, cos_ref, sin_ref, q_ref, k_ref, v_ref = refs
    else:
        x_ref, sh_ref, sc_ref, w_ref, q_ref, k_ref, v_ref, ko_ref, vo_ref = refs
    for rows, proj in _sub_tile_projections(x_ref, sh_ref, sc_ref, w_ref):
        for c in range(2 * C_WIDTH // LANES):
            lo = c * LANES
            x = proj[:, lo:lo + LANES]
            if latent:
                x = _rope(x, cos_ref[rows, :], sin_ref[rows, :])
            if lo < C_WIDTH:
                q_ref[rows, lo:lo + LANES] = (x * Q_SCALE).astype(BF16)
            else:
                k_ref[rows, lo - C_WIDTH:lo - C_WIDTH + LANES] = x.astype(BF16)
        v = proj[:, 2 * C_WIDTH:]
        v_ref[:, rows] = v.T.astype(BF16)
        if not latent:
            ko_ref[rows, :] = proj[:, C_WIDTH:2 * C_WIDTH]
            vo_ref[rows, :] = v


def _in_odd(x, mod, layer, w_in, rope, *, seq_len, latent):
    rows = x.shape[0]
    tm = PROJ_TILE
    tps = max(seq_len // tm, 1)
    row_spec = lambda w: pl.BlockSpec((tm, w), lambda i: (i, 0))
    in_specs = [row_spec(D_MODEL), _mod_spec(layer, 0, tps, latent), _mod_spec(layer, 1, tps, latent),
                _layer_spec(w_in, layer // 2)]
    args = [x, mod, mod, w_in]
    out_specs = [row_spec(C_WIDTH)] * 2 + [pl.BlockSpec((C_WIDTH, tm), lambda i: (0, i))]
    out_shape = [jax.ShapeDtypeStruct((rows, C_WIDTH), BF16)] * 2 + [jax.ShapeDtypeStruct((C_WIDTH, rows), BF16)]
    if latent:
        pos_spec = pl.BlockSpec((tm, LANES), lambda i: (i % tps, 0))
        in_specs += [pos_spec, pos_spec]
        args += list(rope)
    else:
        out_specs += [row_spec(C_WIDTH)] * 2
        out_shape += [jax.ShapeDtypeStruct((rows, C_WIDTH), F32)] * 2
    return pl.pallas_call(
        functools.partial(_in_odd_kernel, latent=latent),
        grid=(rows // tm,), in_specs=in_specs, out_specs=out_specs, out_shape=out_shape,
        compiler_params=_params("arbitrary"), name="in_odd",
    )(*args)


CONV_HALO = 16


CONV_CHUNK = 128


def _conv_module(pad_ref, s, seq_len, w_ref, b_ref, g_ref, beta_ref, o_ref):
    centre = B_CONV_W // 2
    for r in range(seq_len // CONV_CHUNK):
        base = CONV_HALO + r * CONV_CHUNK
        cols = []
        for c in range(B_WIDTH // LANES):
            lanes = slice(c * LANES, (c + 1) * LANES)
            acc = None
            for b in range(SUBLANES):
                part = None
                for a in range(-2, 2):
                    k = SUBLANES * a + b + centre
                    if not 0 <= k < B_CONV_W:
                        continue
                    start = base + SUBLANES * a
                    term = pad_ref[s, start:start + CONV_CHUNK + SUBLANES, lanes] * w_ref[k:k + 1, lanes]
                    part = term if part is None else part + term
                part = part[b:b + CONV_CHUNK]
                acc = part if acc is None else acc + part
            cols.append(acc + b_ref[:, lanes])
        y = _layer_norm(jnp.concatenate(cols, axis=-1), g_ref[...], beta_ref[...])
        o_ref[s * seq_len + r * CONV_CHUNK:s * seq_len + (r + 1) * CONV_CHUNK, :] = (y * _sigmoid(y)).astype(BF16)


KEY_CHUNK = MXU_DIM
SCORE_LOOKAHEAD = 8


def _half_lanes(x, half):
    lane = lax.broadcasted_iota(jnp.int32, x.shape, 1)
    return jnp.where((lane < HEAD_DIM) == (half == 0), x, jnp.zeros_like(x))


def _attend(n_items, n_keys, v_dim, query, key_chunk, value_chunk):
    n_chunks = n_keys // KEY_CHUNK
    steps = [(i, c) for i in range(n_items) for c in range(n_chunks)]
    queries = {}

    def scores(step):
        i, c = step
        if i not in queries:
            queries[i] = query(i)
        return _dot_nt(key_chunk(i, slice(c * KEY_CHUNK, (c + 1) * KEY_CHUNK)), queries[i])

    pending = [scores(st) for st in steps[:SCORE_LOOKAHEAD]]
    outs = []
    for n, (i, c) in enumerate(steps):
        s = pending.pop(0)
        if n + SCORE_LOOKAHEAD < len(steps):
            pending.append(scores(steps[n + SCORE_LOOKAHEAD]))
        vt = value_chunk(i, slice(c * KEY_CHUNK, (c + 1) * KEY_CHUNK))
        m = s.max(axis=0, keepdims=True)
        if c == 0:
            o = _dot(vt, jnp.exp2(s - m).astype(BF16))
        else:
            m = jnp.maximum(m_run, m)
            o = o_run * jnp.exp2(m_run - m) + _dot(vt, jnp.exp2(s - m).astype(BF16))
        m_run, o_run = m, o
        if c == n_chunks - 1:
            queries.pop(i)
            outs.append(o[0:v_dim] * (1.0 / o[v_dim:v_dim + 1]))
    return outs


A_VT_ROWS = HEAD_DIM + BF16_ROWS
C_VT_ROWS = C_VDIM + BF16_ROWS


def _attn_even_kernel(*refs, latent, seq_len):
    if latent:
        q_ref, k_ref, vt_ref, ck_ref, cv_ref, o_ref, kall_ref, vtall_ref = refs
    else:
        q_ref, k_ref, vt_ref, o_ref, vtall_ref = refs
        kall_ref = k_ref
    n_keys = vtall_ref.shape[1]

    @pl.when(pl.program_id(1) == 0)
    def _():
        if latent:
            kall_ref[0:seq_len, :] = k_ref[...]
            ck = ck_ref[...]
            kall_ref[seq_len:, 0:LANES] = ck.astype(BF16)
            kall_ref[seq_len:, LANES:2 * LANES] = pltpu.roll(ck, HEAD_DIM, 1).astype(BF16)
            cvt = cv_ref[...].T
        for j in range(A_KV_HEADS):
            r0 = j * A_VT_ROWS
            vtall_ref[r0:r0 + HEAD_DIM, 0:seq_len] = vt_ref[j * HEAD_DIM:(j + 1) * HEAD_DIM, :]
            if latent:
                vtall_ref[r0:r0 + HEAD_DIM, seq_len:] = cvt[j * HEAD_DIM:(j + 1) * HEAD_DIM].astype(BF16)
            vtall_ref[r0 + HEAD_DIM:r0 + A_VT_ROWS, :] = jnp.ones((BF16_ROWS, n_keys), BF16)

    group = A_HEADS // A_KV_HEADS

    def query(h):
        lo = (h // 2) * LANES
        return _half_lanes(q_ref[:, lo:lo + LANES], h % 2)

    def key_chunk(h, rows):
        order = 0 if h % 2 == h // group else 1
        return kall_ref[rows, order * LANES:(order + 1) * LANES]

    def value_chunk(h, cols):
        r0 = (h // group) * A_VT_ROWS
        return vtall_ref[r0:r0 + A_VT_ROWS, cols]

    outs = _attend(A_HEADS, n_keys, HEAD_DIM, query, key_chunk, value_chunk)
    o_ref[...] = jnp.concatenate(outs, axis=0).T.astype(BF16)


def _attn_even(q, k2, vt, cache, *, seq_len, latent):
    rows = q.shape[0]
    tq = QUERY_TILE
    tps = seq_len // tq
    q_spec = pl.BlockSpec((tq, A_WIDTH), lambda b, i: (b * tps + i, 0))
    in_specs = [q_spec, pl.BlockSpec((seq_len, 2 * A_KV_WIDTH), lambda b, i: (b, 0)),
                pl.BlockSpec((A_KV_WIDTH, seq_len), lambda b, i: (0, b))]
    args = [q, k2, vt]
    n_keys = seq_len
    scratch = []
    if latent:
        past = cache[0].shape[1]
        n_keys += past
        c_spec = pl.BlockSpec((None, past, A_KV_WIDTH), lambda b, i: (b, 0, 0))
        in_specs += [c_spec, c_spec]
        args += [cache[0].reshape(-1, past, A_KV_WIDTH), cache[1].reshape(-1, past, A_KV_WIDTH)]
        scratch.append(pltpu.VMEM((n_keys, 2 * A_KV_WIDTH), BF16))
    scratch.append(pltpu.VMEM((A_KV_HEADS * A_VT_ROWS, n_keys), BF16))
    return pl.pallas_call(
        functools.partial(_attn_even_kernel, latent=latent, seq_len=seq_len),
        grid=(rows // seq_len, tps), in_specs=in_specs, out_specs=q_spec,
        out_shape=jax.ShapeDtypeStruct((rows, A_WIDTH), BF16), scratch_shapes=scratch,
        compiler_params=_params("arbitrary", "arbitrary"), name="attn_even",
    )(*args)


def _attn_odd_kernel(*refs, latent, lam_init, seq_len):
    if latent:
        q_ref, k_ref, vt_ref, ck_ref, cv_ref, dl_ref, sg_ref, o_ref, kall_ref, vtall_ref = refs
    else:
        q_ref, k_ref, vt_ref, dl_ref, sg_ref, o_ref, vtall_ref = refs
        kall_ref = k_ref
    n_keys = vtall_ref.shape[1]
    past = n_keys - seq_len

    @pl.when(pl.program_id(1) == 0)
    def _():
        if latent:
            kall_ref[0:seq_len, :] = k_ref[...]
        for h in range(C_HEADS):
            lanes = slice(h * C_VDIM, (h + 1) * C_VDIM)
            r0 = h * C_VT_ROWS
            vtall_ref[r0:r0 + C_VDIM, 0:seq_len] = vt_ref[lanes, :]
            if latent:
                kall_ref[seq_len:, lanes] = ck_ref[pl.ds(h, past, stride=C_HEADS), :].astype(BF16)
                vtall_ref[r0:r0 + C_VDIM, seq_len:] = cv_ref[pl.ds(h, past, stride=C_HEADS), :].T.astype(BF16)
            vtall_ref[r0 + C_VDIM:r0 + C_VT_ROWS, :] = jnp.ones((BF16_ROWS, n_keys), BF16)

    dl = dl_ref[...]
    lam = (jnp.exp(jnp.sum(dl[0:1] * dl[1:2], axis=-1, keepdims=True))
           - jnp.exp(jnp.sum(dl[2:3] * dl[3:4], axis=-1, keepdims=True)) + lam_init)

    def lanes_of(i):
        return slice((i // 2) * C_VDIM, (i // 2 + 1) * C_VDIM)

    def query(i):
        return _half_lanes(q_ref[:, lanes_of(i)], i % 2)

    def key_chunk(i, rows):
        return kall_ref[rows, lanes_of(i)]

    def value_chunk(i, cols):
        r0 = (i // 2) * C_VT_ROWS
        return vtall_ref[r0:r0 + C_VT_ROWS, cols]

    outs = _attend(2 * C_HEADS, n_keys, C_VDIM, query, key_chunk, value_chunk)
    for h in range(C_HEADS):
        o = outs[2 * h] - lam * outs[2 * h + 1]
        ms = jnp.mean(o * o, axis=0, keepdims=True)
        o = o * lax.rsqrt(ms + EPS) * sg_ref[...] * (1.0 - lam_init)
        o_ref[:, h * C_VDIM:(h + 1) * C_VDIM] = o.T.astype(BF16)


def _attn_odd(q, k, vt, cache, dl, sg, layer, *, seq_len, latent):
    rows = q.shape[0]
    tq = QUERY_TILE
    tps = seq_len // tq
    lam_init = 0.8 - 0.6 * math.exp(-0.3 * layer)
    q_spec = pl.BlockSpec((tq, C_WIDTH), lambda b, i: (b * tps + i, 0))
    in_specs = [q_spec, pl.BlockSpec((seq_len, C_WIDTH), lambda b, i: (b, 0)),
                pl.BlockSpec((C_WIDTH, seq_len), lambda b, i: (0, b))]
    args = [q, k, vt]
    n_keys = seq_len
    scratch = []
    if latent:
        past = cache[0].shape[1]
        n_keys += past
        c_spec = pl.BlockSpec((None, past * C_HEADS, C_VDIM), lambda b, i: (b, 0, 0))
        in_specs += [c_spec, c_spec]
        args += [cache[0].reshape(-1, past * C_HEADS, C_VDIM), cache[1].reshape(-1, past * C_HEADS, C_VDIM)]
        scratch.append(pltpu.VMEM((n_keys, C_WIDTH), BF16))
    scratch.append(pltpu.VMEM((C_HEADS * C_VT_ROWS, n_keys), BF16))
    in_specs += [_layer_spec(dl, layer // 2), _layer_spec(sg, layer // 2)]
    args += [dl, sg]
    return pl.pallas_call(
        functools.partial(_attn_odd_kernel, latent=latent, lam_init=lam_init, seq_len=seq_len),
        grid=(rows // seq_len, tps), in_specs=in_specs, out_specs=q_spec,
        out_shape=jax.ShapeDtypeStruct((rows, C_WIDTH), BF16), scratch_shapes=scratch,
        compiler_params=_params("arbitrary", "arbitrary"), name="attn_odd",
    )(*args)


def _neighbour_rows(u):
    n = u.shape[0]
    sub = lax.broadcasted_iota(jnp.int32, (SUBLANES, u.shape[1]), 0)
    prev = pltpu.roll(u, 1, 0)
    nxt = pltpu.roll(u, n - 1, 0)
    prev = jnp.concatenate([jnp.where(sub == 0, 0.0, prev[0:SUBLANES]), prev[SUBLANES:]], axis=0)
    nxt = jnp.concatenate([nxt[:n - SUBLANES], jnp.where(sub == SUBLANES - 1, 0.0, nxt[n - SUBLANES:])], axis=0)
    return prev, nxt


def _mix_ffn_kernel(*refs, n_parts, seq_len):
    parts = refs[:n_parts]
    (wo_ref, x_ref, g1_ref, sh_ref, sc_ref, g2_ref, lng_ref, lnb_ref, wu_ref, cw_ref, cb_ref, wd_ref,
     o_ref, x1_ref, a_ref) = refs[n_parts:]
    width = wo_ref.shape[0] // n_parts
    spans = [slice(r, r + seq_len) for r in range(0, x_ref.shape[0], seq_len)]

    def conv(u, lo):
        cols = slice(lo, lo + FF_CHUNK)
        prev, nxt = _neighbour_rows(u)
        return prev * cw_ref[0:1, cols] + u * cw_ref[1:2, cols] + nxt * cw_ref[2:3, cols] + cb_ref[:, cols]

    def blocks(rows):
        return [slice(r, r + NORM_BLOCK) for r in range(rows.start, rows.stop, NORM_BLOCK)]

    hs = []
    for rows in spans:
        for blk in blocks(rows):
            mix = _dot(parts[0][blk, :], wo_ref[0:width, :])
            for j in range(1, n_parts):
                mix = mix + _dot(parts[j][blk, :], wo_ref[j * width:(j + 1) * width, :])
            x1_ref[blk, :] = _layer_norm(ALPHA * x_ref[blk, :] + g1_ref[...] * mix, lng_ref[0:1, :], lnb_ref[0:1, :])
        hs.append((x1_ref[rows, :] * (1.0 + sc_ref[...]) + sh_ref[...]).astype(BF16))
    for rows, h in zip(spans, hs):
        for j in range(FF_CHUNKS):
            lo = j * FF_CHUNK
            gate = conv(_dot(h, wu_ref[:, lo:lo + FF_CHUNK]), lo)
            val = conv(_dot(h, wu_ref[:, D_FF + lo:D_FF + lo + FF_CHUNK]), D_FF + lo)
            a_ref[rows, lo:lo + FF_CHUNK] = (gate * _sigmoid(gate) * val).astype(BF16)
        for blk in blocks(rows):
            y = ALPHA * x1_ref[blk, :] + g2_ref[...] * _dot(a_ref[blk, :], wd_ref[...])
            o_ref[blk, :] = _layer_norm(y, lng_ref[1:2, :], lnb_ref[1:2, :])


def _mix_ffn(parts, w_out, x, mod, layer, ln_g, ln_b, w_up, cw, cb, w_down, *, seq_len, latent):
    rows = x.shape[0]
    tm = seq_len if latent else FFN_CONTEXT_SEQS * seq_len
    row_spec = lambda w: pl.BlockSpec((tm, w), lambda i: (i, 0))
    res = lambda a, l: _layer_spec(a, l, single_buffer=True)
    in_specs = [row_spec(p.shape[1]) for p in parts]
    in_specs += [res(w_out, layer // 2), row_spec(D_MODEL), _mod_spec(layer, 2, 1, latent),
                 _mod_spec(layer, 3, 1, latent), _mod_spec(layer, 4, 1, latent), _mod_spec(layer, 5, 1, latent),
                 res(ln_g, layer), res(ln_b, layer), res(w_up, layer), res(cw, layer), res(cb, layer),
                 res(w_down, layer)]
    return pl.pallas_call(
        functools.partial(_mix_ffn_kernel, n_parts=len(parts), seq_len=seq_len),
        grid=(rows // tm,), in_specs=in_specs, out_specs=row_spec(D_MODEL),
        out_shape=jax.ShapeDtypeStruct((rows, D_MODEL), F32),
        scratch_shapes=[pltpu.VMEM((tm, D_MODEL), F32), pltpu.VMEM((tm, D_FF), BF16)],
        compiler_params=_params("arbitrary"), name="mix_ffn",
    )(*parts, w_out, x, mod, mod, mod, mod, ln_g, ln_b, w_up, cw, cb, w_down)


def _rope_lane_tables(n):
    t = jnp.arange(n)
    half = HEAD_DIM // 2
    inv = 1.0 / (ROPE_THETA ** (jnp.arange(0, half, 2, dtype=F32) / half))
    ang_r = (t // GRID_W).astype(F32)[:, None] * inv
    ang_c = (t % GRID_W).astype(F32)[:, None] * inv
    cos = jnp.concatenate([jnp.cos(ang_r)] * 2 + [jnp.cos(ang_c)] * 2, axis=-1)
    sin = jnp.concatenate([-jnp.sin(ang_r), jnp.sin(ang_r), -jnp.sin(ang_c), jnp.sin(ang_c)], axis=-1)
    reps = LANES // HEAD_DIM
    return jnp.tile(cos, (1, reps)), jnp.tile(sin, (1, reps))


def kernel(x_prompt, x_sample, cache_a_k_l0, cache_a_v_l0, cache_c_k_l1, cache_c_v_l1, cache_a_k_l2,
           cache_a_v_l2, cache_c_k_l3, cache_c_v_l3, c, c_ctx, w_ada, b_ada, ln_g, ln_b, w_in_even,
           q_norm_g, k_norm_g, b_conv_w, b_conv_b, b_norm_g, b_norm_b, w_out_even, w_in_odd,
           diff_lambda, subln_g, w_out_odd, w_up, ffn_conv_w, ffn_conv_b, w_down):
    batch, seq, d = x_prompt.shape
    dec_batch, dec_seq, _ = x_sample.shape
    assert d == D_MODEL and 1 + dec_batch <= MOD_ROWS
    caches = [(cache_a_k_l0, cache_a_v_l0), (cache_c_k_l1, cache_c_v_l1),
              (cache_a_k_l2, cache_a_v_l2), (cache_c_k_l3, cache_c_v_l3)]

    cond = jnp.zeros((MOD_ROWS, d), F32).at[0].set(c_ctx).at[1:1 + dec_batch].set(c)
    mod = _modulation(cond, w_ada, b_ada).reshape(DEPTH * MOD_ROWS * 6, 1, d)
    rope = _rope_lane_tables(dec_seq)

    w_in_even_b = w_in_even.astype(BF16)
    w_out_even_b = w_out_even.astype(BF16)
    w_in_odd_b = w_in_odd.astype(BF16)
    w_out_odd_b = w_out_odd.astype(BF16)
    w_up_b = w_up.astype(BF16)
    w_down_b = w_down.astype(BF16)
    cb = ffn_conv_b[:, None, :]
    qg = jnp.tile(q_norm_g, (1, LANES // HEAD_DIM))
    kg = jnp.tile(k_norm_g, (1, LANES // HEAD_DIM))
    conv_b = b_conv_b[:, None, :]
    norm_g = b_norm_g[:, None, :]
    norm_b = b_norm_b[:, None, :]
    sub_g = subln_g[:, :, None]

    streams = [
        [x_prompt.reshape(batch * seq, d), seq, False],
        [x_sample.reshape(dec_batch * dec_seq, d), dec_seq, True],
    ]
    ctx_state = []
    for layer in range(DEPTH):
        for stream in streams:
            x, s, latent = stream
            kw = dict(seq_len=s, latent=latent)
            if layer % 2 == 0:
                e = layer // 2
                res = _in_even(x, mod, layer, w_in_even_b, qg[e][None], kg[e][None],
                               (b_conv_w, conv_b, norm_g, norm_b), rope, **kw)
                q, k2, vt, conv = res[:4]
                if not latent:
                    ctx_state.append((res[4].reshape(batch, seq, A_KV_HEADS, HEAD_DIM),
                                      res[5].reshape(batch, seq, A_KV_HEADS, HEAD_DIM)))
                attn = _attn_even(q, k2, vt, caches[layer], **kw)
                parts, w_out = [attn, conv], w_out_even_b
            else:
                res = _in_odd(x, mod, layer, w_in_odd_b, rope, **kw)
                q, k, vt = res[:3]
                if not latent:
                    ctx_state.append((res[3].reshape(batch, seq, C_HEADS, C_VDIM),
                                      res[4].reshape(batch, seq, C_HEADS, C_VDIM)))
                attn = _attn_odd(q, k, vt, caches[layer], diff_lambda, sub_g, layer, **kw)
                parts, w_out = [attn], w_out_odd_b
            stream[0] = _mix_ffn(parts, w_out, x, mod, layer, ln_g, ln_b, w_up_b, ffn_conv_w, cb, w_down_b, **kw)
    y_prompt = streams[0][0].reshape(batch, seq, d)
    y_sample = streams[1][0].reshape(dec_batch, dec_seq, d)
    (a_k0, a_v0), (c_k1, c_v1), (a_k2, a_v2), (c_k3, c_v3) = ctx_state
    return (y_prompt, y_sample, a_k0, a_v0, c_k1, c_v1, a_k2, a_v2, c_k3, c_v3)
```
